```python
import math
import jax, jax.numpy as jnp
from jax import lax
import numpy as np

D_MODEL = 2048
BATCH = 4
SEQ = 4096
DEPTH = 4

MLA_HEADS = 6
MLA_Q_LORA = 512
MLA_KV_LORA = 512
MLA_NOPE = 128
MLA_ROPE = 64
MLA_V = 128
ROPE_THETA = 10000.0
DIFF_HEADS = 6
DIFF_D = 64
DIFF_V = 2 * DIFF_D
DIL_HEADS = 6
DIL_D = 128
DIL_PATTERNS = ((128, 1), (512, 4), (2048, 16))
DIL_BLOCK = 128
N_BRANCH = 3
D_FF = 5632
CONV_W = 3
Q_BLOCK = 128
EPS = 1e-6

IN_SIZES = (MLA_Q_LORA, MLA_KV_LORA, MLA_ROPE,
            DIFF_HEADS * 2 * DIFF_D, DIFF_HEADS * 2 * DIFF_D, DIFF_HEADS * DIFF_V,
            DIL_HEADS * DIL_D, DIL_HEADS * DIL_D, DIL_HEADS * DIL_D,
            N_BRANCH * D_MODEL)
D_IN = (MLA_Q_LORA + MLA_KV_LORA + MLA_ROPE + DIFF_HEADS * (5 * DIFF_D + DIFF_V - DIFF_D)
        + 3 * DIL_HEADS * DIL_D + N_BRANCH * D_MODEL)

kernel_name = "hybrid_mla_diff_dilated_convffn"


def rms_norm(x, g):
    xf = x.astype(jnp.float32)
    y = xf * lax.rsqrt(jnp.mean(xf * xf, axis=-1, keepdims=True) + EPS)
    return (y * g.astype(jnp.float32)).astype(x.dtype)


def heads(t, n):
    B, S, _ = t.shape
    return t.reshape(B, S, n, -1).transpose(0, 2, 1, 3)


def merge_heads(t):
    B, H, S, d = t.shape
    return t.transpose(0, 2, 1, 3).reshape(B, S, H * d)


def rope(t, positions):
    half = t.shape[-1] // 2
    inv = ROPE_THETA ** (-jnp.arange(half, dtype=jnp.float32) / half)
    ang = positions.astype(jnp.float32)[:, None] * inv[None, :]
    cos, sin = jnp.cos(ang), jnp.sin(ang)
    tf = t.astype(jnp.float32)
    t1, t2 = tf[..., :half], tf[..., half:]
    return jnp.concatenate([t1 * cos - t2 * sin, t1 * sin + t2 * cos], axis=-1).astype(t.dtype)


def alibi_slopes(n):
    return 2.0 ** (-8.0 * jnp.arange(1, n + 1, dtype=jnp.float32) / n)


def causal_softmax_attention(q, k, v, scale):
    B, H, S, dq = q.shape
    nb = S // Q_BLOCK
    qb = q.reshape(B, H, nb, Q_BLOCK, dq).transpose(2, 0, 1, 3, 4)
    kpos = jnp.arange(S)

    def block(args):
        qi, bi = args
        qpos = bi * Q_BLOCK + jnp.arange(Q_BLOCK)
        s = jnp.einsum('bhqd,bhkd->bhqk', qi, k, preferred_element_type=jnp.float32) * scale
        s = jnp.where(kpos[None, :] <= qpos[:, None], s, -jnp.inf)
        p = jax.nn.softmax(s, axis=-1)
        return jnp.einsum('bhqk,bhkd->bhqd', p.astype(v.dtype), v)

    out = lax.map(block, (qb, jnp.arange(nb)))
    return out.transpose(1, 2, 0, 3, 4).reshape(B, H, S, v.shape[-1])


def causal_differential_attention(q1, q2, k1, k2, v, lam, slopes, scale):
    B, H, S, d = q1.shape
    nb = S // Q_BLOCK

    def blocks(t):
        return t.reshape(B, H, nb, Q_BLOCK, d).transpose(2, 0, 1, 3, 4)

    kpos = jnp.arange(S)

    def block(args):
        q1i, q2i, bi = args
        qpos = bi * Q_BLOCK + jnp.arange(Q_BLOCK)
        dist = (qpos[:, None] - kpos[None, :]).astype(jnp.float32)
        causal = dist >= 0
        bias = -slopes[:, None, None] * dist

        def attn_map(qi, k):
            s = jnp.einsum('bhqd,bhkd->bhqk', qi, k, preferred_element_type=jnp.float32) * scale + bias
            return jax.nn.softmax(jnp.where(causal, s, -jnp.inf), axis=-1)

        a = attn_map(q1i, k1) - lam * attn_map(q2i, k2)
        return jnp.einsum('bhqk,bhkd->bhqd', a.astype(v.dtype), v)

    out = lax.map(block, (blocks(q1), blocks(q2), jnp.arange(nb)))
    return out.transpose(1, 2, 0, 3, 4).reshape(B, H, S, v.shape[-1])


def dilated_window_attention(q, k, v, slopes, scale, window, dilation):
    B, H, S, d = q.shape
    dv = v.shape[-1]
    n_back = window // dilation
    L = S // dilation
    nb = -(-L // DIL_BLOCK)
    Lp = nb * DIL_BLOCK

    def sub(t):
        return t.reshape(B, H, L, dilation, t.shape[-1]).transpose(0, 1, 3, 2, 4)

    qs = jnp.pad(sub(q), ((0, 0), (0, 0), (0, 0), (0, Lp - L), (0, 0)))
    qs = qs.reshape(B, H, dilation, nb, DIL_BLOCK, d)

    def key_blocks(t):
        tp = jnp.pad(sub(t), ((0, 0), (0, 0), (0, 0), (DIL_BLOCK, Lp - L), (0, 0)))
        tb = tp.reshape(B, H, dilation, nb + 1, DIL_BLOCK, t.shape[-1])
        return jnp.concatenate([tb[:, :, :, :-1], tb[:, :, :, 1:]], axis=-2)

    kb, vb = key_blocks(k), key_blocks(v)
    i = jnp.arange(DIL_BLOCK)[:, None]
    j = jnp.arange(2 * DIL_BLOCK)[None, :]
    steps = i + DIL_BLOCK - j
    key_sub = jnp.arange(nb)[:, None, None] * DIL_BLOCK - DIL_BLOCK + j[None]
    valid = (steps >= 0) & (steps <= n_back) & (key_sub >= 0)
    bias = -slopes[None, :, None, None, None, None] * (steps * dilation).astype(jnp.float32)

    s = jnp.einsum('bhrnqd,bhrnkd->bhrnqk', qs, kb, preferred_element_type=jnp.float32) * scale + bias
    s = jnp.where(valid, s, -jnp.inf)
    m = jnp.max(s, axis=-1, keepdims=True)
    e = jnp.exp(s - m)
    l = jnp.sum(e, axis=-1, keepdims=True)
    o = jnp.einsum('bhrnqk,bhrnkd->bhrnqd', (e / l).astype(v.dtype), vb)
    lse = m + jnp.log(l)

    def unblock(t):
        c = t.shape[-1]
        t = t.reshape(B, H, dilation, Lp, c)[:, :, :, :L]
        return t.transpose(0, 1, 3, 2, 4).reshape(B, H, S, c)

    return unblock(o), unblock(lse)[..., 0]


def dilated_mixture(q, k, v, slopes, scale):
    outs, lses = [], []
    for window, dilation in DIL_PATTERNS:
        o, lse = dilated_window_attention(q, k, v, slopes, scale, window, dilation)
        outs.append(o)
        lses.append(lse)
    w = jax.nn.softmax(jnp.stack(lses, axis=0), axis=0)
    return jnp.einsum('pbhs,pbhsd->bhsd', w.astype(v.dtype), jnp.stack(outs, axis=0))


def hybrid_mixer(x, layer, attn_norm_g, w_in, mla_q_lora_norm_g, mla_kv_lora_norm_g,
                 mla_w_uq, mla_w_uk, mla_w_uv, mla_q_gain, mla_k_gain,
                 diff_q_gain, diff_k_gain, diff_lam_q1, diff_lam_k1, diff_lam_q2, diff_lam_k2,
                 diff_out_norm_g, dil_q_gain, dil_k_gain, gate_bias,
                 w_br_mla, w_br_diff, w_br_dil, w_out):
    B, S, _ = x.shape
    pos = jnp.arange(S)
    h = rms_norm(x, attn_norm_g)
    z = h @ w_in
    split_at = np.cumsum(IN_SIZES)[:-1].tolist()
    c_q, c_kv, k_r, dq, dk, dv, lq, lk, lv, gz = jnp.split(z, split_at, axis=-1)

    q = heads(rms_norm(c_q, mla_q_lora_norm_g) @ mla_w_uq, MLA_HEADS)
    ckv = rms_norm(c_kv, mla_kv_lora_norm_g)
    k_nope = rms_norm(heads(ckv @ mla_w_uk, MLA_HEADS), mla_k_gain[:MLA_NOPE])
    v_mla = heads(ckv @ mla_w_uv, MLA_HEADS)
    q_nope = rms_norm(q[..., :MLA_NOPE], mla_q_gain[:MLA_NOPE])
    q_rope = rope(rms_norm(q[..., MLA_NOPE:], mla_q_gain[MLA_NOPE:]), pos)
    k_rope = rope(rms_norm(k_r, mla_k_gain[MLA_NOPE:]), pos)
    q_m = jnp.concatenate([q_nope, q_rope], axis=-1)
    k_m = jnp.concatenate([k_nope, jnp.broadcast_to(k_rope[:, None], (B, MLA_HEADS, S, MLA_ROPE))], axis=-1)
    o_mla = causal_softmax_attention(q_m, k_m, v_mla, (MLA_NOPE + MLA_ROPE) ** -0.5)

    slopes = alibi_slopes(DIFF_HEADS + DIL_HEADS)

    qd = rms_norm(dq.reshape(B, S, DIFF_HEADS, 2, DIFF_D), diff_q_gain).transpose(0, 2, 3, 1, 4)
    kd = rms_norm(dk.reshape(B, S, DIFF_HEADS, 2, DIFF_D), diff_k_gain).transpose(0, 2, 3, 1, 4)
    v_d = heads(dv, DIFF_HEADS)
    lam_init = 0.8 - 0.6 * math.exp(-0.3 * layer)
    lam = (jnp.exp(jnp.sum(diff_lam_q1.astype(jnp.float32) * diff_lam_k1.astype(jnp.float32)))
           - jnp.exp(jnp.sum(diff_lam_q2.astype(jnp.float32) * diff_lam_k2.astype(jnp.float32)))
           + lam_init)
    o_diff = causal_differential_attention(qd[:, :, 0], qd[:, :, 1], kd[:, :, 0], kd[:, :, 1], v_d,
                                           lam, slopes[0::2], DIFF_D ** -0.5)
    o_diff = rms_norm(o_diff, diff_out_norm_g) * (1.0 - lam_init)

    q_l = rms_norm(heads(lq, DIL_HEADS), dil_q_gain)
    k_l = rms_norm(heads(lk, DIL_HEADS), dil_k_gain)
    v_l = heads(lv, DIL_HEADS)
    o_dil = dilated_mixture(q_l, k_l, v_l, slopes[1::2], DIL_D ** -0.5)

    gates = jax.nn.sigmoid(gz.reshape(B, S, N_BRANCH, D_MODEL) + gate_bias)
    merged = (gates[:, :, 0] * (merge_heads(o_mla) @ w_br_mla)
              + gates[:, :, 1] * (merge_heads(o_diff) @ w_br_diff)
              + gates[:, :, 2] * (merge_heads(o_dil) @ w_br_dil))
    return x + merged @ w_out


def conv_ffn(x, ffn_norm_g, ffn_w_up, ffn_conv_w, ffn_conv_b, ffn_w_down):
    h = rms_norm(x, ffn_norm_g)
    u = h @ ffn_w_up
    C = u.shape[-1]
    u = lax.conv_general_dilated(u, ffn_conv_w[:, None, :], window_strides=(1,),
                                 padding=[(CONV_W - 1, 0)],
                                 dimension_numbers=('NWC', 'WIO', 'NWC'),
                                 feature_group_count=C) + ffn_conv_b
    gate, up = jnp.split(u, 2, axis=-1)
    return x + (jax.nn.silu(gate) * up) @ ffn_w_down


def setup_inputs(seed: int = 0) -> dict:
    key = jax.random.key(seed)
    ks = iter(jax.random.split(key, 40))
    L = DEPTH
    f32 = jnp.float32

    def w(shape, fan_in):
        return jax.random.normal(next(ks), shape, f32) * fan_in ** -0.5

    def gain(shape):
        return 1.0 + 0.02 * jax.random.normal(next(ks), shape, f32)

    def small(shape, s):
        return s * jax.random.normal(next(ks), shape, f32)

    return {
        "x": jax.random.normal(next(ks), (BATCH, SEQ, D_MODEL), f32),
        "attn_norm_g": gain((L, D_MODEL)),
        "w_in": w((L, D_MODEL, D_IN), D_MODEL),
        "mla_q_lora_norm_g": gain((L, MLA_Q_LORA)),
        "mla_kv_lora_norm_g": gain((L, MLA_KV_LORA)),
        "mla_w_uq": w((L, MLA_Q_LORA, MLA_HEADS * (MLA_NOPE + MLA_ROPE)), MLA_Q_LORA),
        "mla_w_uk": w((L, MLA_KV_LORA, MLA_HEADS * MLA_NOPE), MLA_KV_LORA),
        "mla_w_uv": w((L, MLA_KV_LORA, MLA_HEADS * MLA_V), MLA_KV_LORA),
        "mla_q_gain": gain((L, MLA_NOPE + MLA_ROPE)),
        "mla_k_gain": gain((L, MLA_NOPE + MLA_ROPE)),
        "diff_q_gain": gain((L, DIFF_D)),
        "diff_k_gain": gain((L, DIFF_D)),
        "diff_lam_q1": small((L, DIFF_D), 0.1),
        "diff_lam_k1": small((L, DIFF_D), 0.1),
        "diff_lam_q2": small((L, DIFF_D), 0.1),
        "diff_lam_k2": small((L, DIFF_D), 0.1),
        "diff_out_norm_g": gain((L, DIFF_V)),
        "dil_q_gain": gain((L, DIL_D)),
        "dil_k_gain": gain((L, DIL_D)),
        "gate_bias": small((L, N_BRANCH, D_MODEL), 0.01),
        "w_br_mla": w((L, MLA_HEADS * MLA_V, D_MODEL), MLA_HEADS * MLA_V),
        "w_br_diff": w((L, DIFF_HEADS * DIFF_V, D_MODEL), DIFF_HEADS * DIFF_V),
        "w_br_dil": w((L, DIL_HEADS * DIL_D, D_MODEL), DIL_HEADS * DIL_D),
        "w_out": w((L, D_MODEL, D_MODEL), D_MODEL),
        "ffn_norm_g": gain((L, D_MODEL)),
        "ffn_w_up": w((L, D_MODEL, 2 * D_FF), D_MODEL),
        "ffn_conv_w": w((L, CONV_W, 2 * D_FF), CONV_W),
        "ffn_conv_b": small((L, 2 * D_FF), 0.01),
        "ffn_w_down": w((L, D_FF, D_MODEL), D_FF),
    }


def reference(x, attn_norm_g, w_in, mla_q_lora_norm_g, mla_kv_lora_norm_g, mla_w_uq, mla_w_uk,
              mla_w_uv, mla_q_gain, mla_k_gain, diff_q_gain, diff_k_gain, diff_lam_q1,
              diff_lam_k1, diff_lam_q2, diff_lam_k2, diff_out_norm_g, dil_q_gain, dil_k_gain,
              gate_bias, w_br_mla, w_br_diff, w_br_dil, w_out, ffn_norm_g, ffn_w_up,
              ffn_conv_w, ffn_conv_b, ffn_w_down):
    for l in range(DEPTH):
        x = hybrid_mixer(x, l, attn_norm_g[l], w_in[l], mla_q_lora_norm_g[l], mla_kv_lora_norm_g[l],
                         mla_w_uq[l], mla_w_uk[l], mla_w_uv[l], mla_q_gain[l], mla_k_gain[l],
                         diff_q_gain[l], diff_k_gain[l], diff_lam_q1[l], diff_lam_k1[l],
                         diff_lam_q2[l], diff_lam_k2[l], diff_out_norm_g[l], dil_q_gain[l],
                         dil_k_gain[l], gate_bias[l], w_br_mla[l], w_br_diff[l], w_br_dil[l],
                         w_out[l])
        x = conv_ffn(x, ffn_norm_g[l], ffn_w_up[l], ffn_conv_w[l], ffn_conv_b[l], ffn_w_down[l])
    return x
```

```python
import functools
import math

import jax
import jax.numpy as jnp
from jax import lax
from jax.experimental import pallas as pl
from jax.experimental.pallas import tpu as pltpu

F32 = jnp.float32
BF16 = jnp.bfloat16
EPS = 1e-6
NEG_INF = float("-inf")

V7X_LANES = 128
V7X_VMEM_BYTES = 64 * 1024 * 1024
BF16_ROWS_PER_VREG = 16

MLA_HEADS = 6
MLA_NOPE = 128
MLA_ROPE = 64
MLA_V = 128
ROPE_THETA = 10000.0
DIFF_HEADS = 6
DIFF_D = 64
DIFF_V = 128
DIL_HEADS = 6
DIL_D = 128
DIL_PATTERNS = ((128, 1), (512, 4), (2048, 16))
DIL_BLOCK = 128
N_BRANCH = 3
CONV_W = 3
HALO = BF16_ROWS_PER_VREG


def _vmem_limit(block_bytes):
    return int(min(V7X_VMEM_BYTES - (4 << 20), block_bytes + (16 << 20)))


def _dot(a, b):
    return jnp.dot(a, b, preferred_element_type=F32)


def _dot_nt(a, b):
    return lax.dot_general(a, b, (((1,), (1,)), ((), ())), preferred_element_type=F32)


def _rms(x, n):
    return x * lax.rsqrt(jnp.sum(x * x, axis=-1, keepdims=True) * (1.0 / n) + EPS)


def _norm_matmul_body(x_ref, g_ref, w_ref, o_ref, h_ref):
    @pl.when(pl.program_id(1) == 0)
    def _():
        x = x_ref[...]
        h_ref[...] = (_rms(x, x.shape[-1]) * g_ref[...]).astype(BF16)

    o_ref[...] = _dot(h_ref[...], w_ref[...]).astype(o_ref.dtype)


def _norm_matmul(x, g, w, *, tm, tn, out_dtype):
    T, K = x.shape
    N = w.shape[1]
    osz = jnp.dtype(out_dtype).itemsize
    est = 2 * tm * K * 4 + tm * K * 2 + 2 * K * tn * 2 + 2 * tm * tn * osz
    return pl.pallas_call(
        _norm_matmul_body,
        grid=(T // tm, N // tn),
        in_specs=[pl.BlockSpec((tm, K), lambda i, j: (i, 0)),
                  pl.BlockSpec((1, K), lambda i, j: (0, 0)),
                  pl.BlockSpec((K, tn), lambda i, j: (0, j))],
        out_specs=pl.BlockSpec((tm, tn), lambda i, j: (i, j)),
        out_shape=jax.ShapeDtypeStruct((T, N), out_dtype),
        scratch_shapes=[pltpu.VMEM((tm, K), BF16)],
        compiler_params=pltpu.CompilerParams(
            dimension_semantics=("arbitrary", "arbitrary"), vmem_limit_bytes=_vmem_limit(est)),
        name="norm_matmul",
    )(x, g, w)


def _rope(xb, c, s):
    return xb * c + pltpu.roll(xb, 64, 1) * s


def _prep_body(dq_ref, dk_ref, dv_ref, lq_ref, lk_ref, lv_ref, cq_ref, ckv_ref, kr_ref,
               cos_ref, sin_ref, gql_ref, gkvl_ref, wuq_ref, wukv_ref,
               gqn_ref, gqr_ref, gkn_ref, gkr_ref, gdq_ref, gdk_ref, glq_ref, glk_ref,
               qm_ref, kn_ref, kro_ref, vm_ref, qd_ref, kd_ref, vd_ref, ql_ref, kl_ref, vl_ref):
    c = cos_ref[...]
    s = sin_ref[...]
    tm = c.shape[0]

    cqn = (_rms(cq_ref[...], cq_ref.shape[-1]) * gql_ref[...]).astype(BF16)
    q = _dot(cqn, wuq_ref[...])
    for h in range(MLA_HEADS):
        a = 2 * V7X_LANES * h
        qm_ref[:, a:a + 128] = (_rms(q[:, a:a + 128], MLA_NOPE) * gqn_ref[...]).astype(BF16)
        qr = _rms(q[:, a + 128:a + 256], MLA_ROPE) * gqr_ref[...]
        qm_ref[:, a + 128:a + 256] = _rope(qr, c, s).astype(BF16)
    ckvn = (_rms(ckv_ref[...], ckv_ref.shape[-1]) * gkvl_ref[...]).astype(BF16)
    kv = _dot(ckvn, wukv_ref[...])
    hk = MLA_HEADS * MLA_NOPE
    for h in range(MLA_HEADS):
        a = V7X_LANES * h
        kn_ref[:, a:a + 128] = (_rms(kv[:, a:a + 128], MLA_NOPE) * gkn_ref[...]).astype(BF16)
    vm_ref[...] = kv[:, hk:].astype(BF16)
    kro_ref[...] = _rope(_rms(kr_ref[...], MLA_ROPE) * gkr_ref[...], c, s).astype(BF16)

    lo = lax.broadcasted_iota(jnp.int32, (tm, V7X_LANES), 1) < DIFF_D

    def half_rms(xh):
        sq = xh * xh
        s_lo = jnp.sum(jnp.where(lo, sq, 0.0), axis=-1, keepdims=True)
        s_hi = jnp.sum(jnp.where(lo, 0.0, sq), axis=-1, keepdims=True)
        inv = jnp.where(lo, lax.rsqrt(s_lo * (1.0 / DIFF_D) + EPS), lax.rsqrt(s_hi * (1.0 / DIFF_D) + EPS))
        return xh * inv

    for h in range(DIFF_HEADS):
        a = V7X_LANES * h
        qd_ref[:, a:a + 128] = (half_rms(dq_ref[:, a:a + 128]) * gdq_ref[...]).astype(BF16)
        kd_ref[:, a:a + 128] = (half_rms(dk_ref[:, a:a + 128]) * gdk_ref[...]).astype(BF16)
    vd_ref[...] = dv_ref[...].astype(BF16)

    for h in range(DIL_HEADS):
        a = V7X_LANES * h
        ql_ref[:, a:a + 128] = _rms(lq_ref[:, a:a + 128], DIL_D) * glq_ref[...]
        kl_ref[:, a:a + 128] = _rms(lk_ref[:, a:a + 128], DIL_D) * glk_ref[...]
    vl_ref[...] = lv_ref[...]


def _prep(z, cos_t, sin_t, vecs, wuq, wukv, *, S, tm, col):
    T = z.shape[0]
    hw = DIFF_HEADS * 128
    nseq = S // tm

    def zspec(width, start):
        assert start % width == 0
        return pl.BlockSpec((tm, width), lambda i, blk=start // width: (i, blk))

    def full(a):
        return pl.BlockSpec(a.shape, lambda i: (0, 0))

    in_specs = [zspec(hw, col["dq"]), zspec(hw, col["dk"]), zspec(hw, col["dv"]),
                zspec(hw, col["lq"]), zspec(hw, col["lk"]), zspec(hw, col["lv"]),
                zspec(512, col["cq"]), zspec(512, col["ckv"]), zspec(128, col["kr"]),
                pl.BlockSpec((tm, 128), lambda i: (i % nseq, 0)),
                pl.BlockSpec((tm, 128), lambda i: (i % nseq, 0))]
    names = ["gql", "gkvl"]
    in_specs += [full(vecs[n]) for n in names] + [full(wuq), full(wukv)]
    names2 = ["gqn", "gqr", "gkn", "gkr", "gdq", "gdk", "glq", "glk"]
    in_specs += [full(vecs[n]) for n in names2]

    def ospec(width):
        return pl.BlockSpec((tm, width), lambda i: (i, 0))

    out_shape = [jax.ShapeDtypeStruct((T, 2 * hw), BF16),
                 jax.ShapeDtypeStruct((T, hw), BF16),
                 jax.ShapeDtypeStruct((T, 128), BF16),
                 jax.ShapeDtypeStruct((T, hw), BF16),
                 jax.ShapeDtypeStruct((T, hw), BF16),
                 jax.ShapeDtypeStruct((T, hw), BF16),
                 jax.ShapeDtypeStruct((T, hw), BF16),
                 jax.ShapeDtypeStruct((T, hw), F32),
                 jax.ShapeDtypeStruct((T, hw), F32),
                 jax.ShapeDtypeStruct((T, hw), F32)]
    out_specs = [ospec(2 * hw), ospec(hw), ospec(128), ospec(hw), ospec(hw), ospec(hw), ospec(hw),
                 ospec(hw), ospec(hw), ospec(hw)]
    est = 2 * tm * (6 * hw + 1024 + 128 + 256) * 4 + 2 * 2 * 512 * 1536 * 2 + 2 * tm * (11 * hw) * 4
    return pl.pallas_call(
        _prep_body,
        grid=(T // tm,),
        in_specs=in_specs,
        out_specs=out_specs,
        out_shape=out_shape,
        compiler_params=pltpu.CompilerParams(
            dimension_semantics=("arbitrary",), vmem_limit_bytes=_vmem_limit(est)),
        name="prep",
    )(z, z, z, z, z, z, z, z, z, cos_t, sin_t, vecs["gql"], vecs["gkvl"], wuq, wukv,
      *[vecs[n] for n in names2])


def _online_update(m_ref, l_ref, acc_ref, s, v):
    m_prev = m_ref[...]
    m_new = jnp.maximum(m_prev, jnp.max(s, axis=-1, keepdims=True))
    alpha = jnp.exp(m_prev - m_new)
    p = jnp.exp(s - m_new)
    l_ref[...] = alpha * l_ref[...] + jnp.sum(p, axis=-1, keepdims=True)
    acc_ref[...] = alpha * acc_ref[...] + _dot(p.astype(BF16), v)
    m_ref[...] = m_new


def _causal_mask(s):
    row = lax.broadcasted_iota(jnp.int32, s.shape, 0)
    colm = lax.broadcasted_iota(jnp.int32, s.shape, 1)
    return jnp.where(row >= colm, s, NEG_INF)


def _mla_attn_body(q_ref, kn_ref, kr_ref, v_ref, o_ref, m_ref, l_ref, acc_ref, *, blk):
    qi = pl.program_id(2)
    q = q_ref[0]
    qn = q[:, :MLA_NOPE]
    qr = q[:, MLA_NOPE:]
    m_ref[...] = jnp.full(m_ref.shape, NEG_INF, F32)
    l_ref[...] = jnp.zeros(l_ref.shape, F32)
    acc_ref[...] = jnp.zeros(acc_ref.shape, F32)

    def step(kj, masked):
        rows = pl.ds(pl.multiple_of(kj * blk, blk), blk)
        s = _dot_nt(qn, kn_ref[0, rows, :]) + _dot_nt(qr, kr_ref[0, rows, :])
        if masked:
            s = _causal_mask(s)
        _online_update(m_ref, l_ref, acc_ref, s, v_ref[0, rows, :])

    def body(kj, carry):
        step(kj, False)
        return carry

    lax.fori_loop(0, qi, body, 0)
    step(qi, True)
    o_ref[0] = (acc_ref[...] * (1.0 / l_ref[...])).astype(o_ref.dtype)


def _mla_attn(qm, kn, kr, vm, *, blk):
    B, S, _ = qm.shape
    est = 2 * (blk * 256 * 2 + 3 * S * 128 * 2 + blk * 128 * 2) + 3 * blk * 128 * 4 + 4 * blk * blk * 4
    return pl.pallas_call(
        functools.partial(_mla_attn_body, blk=blk),
        grid=(B, MLA_HEADS, S // blk),
        in_specs=[pl.BlockSpec((1, blk, 256), lambda b, h, i: (b, i, h)),
                  pl.BlockSpec((1, S, 128), lambda b, h, i: (b, 0, h)),
                  pl.BlockSpec((1, S, 128), lambda b, h, i: (b, 0, 0)),
                  pl.BlockSpec((1, S, 128), lambda b, h, i: (b, 0, h))],
        out_specs=pl.BlockSpec((1, blk, 128), lambda b, h, i: (b, i, h)),
        out_shape=jax.ShapeDtypeStruct((B, S, MLA_HEADS * MLA_V), BF16),
        scratch_shapes=[pltpu.VMEM((blk, 1), F32), pltpu.VMEM((blk, 1), F32), pltpu.VMEM((blk, 128), F32)],
        compiler_params=pltpu.CompilerParams(
            dimension_semantics=("arbitrary", "arbitrary", "arbitrary"), vmem_limit_bytes=_vmem_limit(est)),
        name="mla_attn",
    )(qm, kn, kr, vm)


def _diff_attn_body(sc_ref, q_ref, k_ref, v_ref, lq1_ref, lk1_ref, lq2_ref, lk2_ref, g_ref, o_ref,
                    m1_ref, l1_ref, acc1_ref, m2_ref, l2_ref, acc2_ref, *, blk):
    h = pl.program_id(1)
    qi = pl.program_id(2)
    slope = sc_ref[h]
    lam_init = sc_ref[DIFF_HEADS]
    q = q_ref[0]
    lo = lax.broadcasted_iota(jnp.int32, q.shape, 1) < DIFF_D
    zero = jnp.zeros(q.shape, q.dtype)
    q1 = jnp.where(lo, q, zero)
    q2 = jnp.where(lo, zero, q)
    for m_ref, l_ref, acc_ref in ((m1_ref, l1_ref, acc1_ref), (m2_ref, l2_ref, acc2_ref)):
        m_ref[...] = jnp.full(m_ref.shape, NEG_INF, F32)
        l_ref[...] = jnp.zeros(l_ref.shape, F32)
        acc_ref[...] = jnp.zeros(acc_ref.shape, F32)

    def step(kj, masked):
        rows = pl.ds(pl.multiple_of(kj * blk, blk), blk)
        k = k_ref[0, rows, :]
        v = v_ref[0, rows, :]
        colm = lax.broadcasted_iota(jnp.int32, (1, blk), 1)
        bias = slope * ((kj - qi) * blk + colm).astype(F32)
        s1 = _dot_nt(q1, k) + bias
        s2 = _dot_nt(q2, k) + bias
        if masked:
            s1 = _causal_mask(s1)
            s2 = _causal_mask(s2)
        _online_update(m1_ref, l1_ref, acc1_ref, s1, v)
        _online_update(m2_ref, l2_ref, acc2_ref, s2, v)

    def body(kj, carry):
        step(kj, False)
        return carry

    lax.fori_loop(0, qi, body, 0)
    step(qi, True)

    lam = (jnp.exp(jnp.sum(lq1_ref[...] * lk1_ref[...], axis=-1, keepdims=True))
           - jnp.exp(jnp.sum(lq2_ref[...] * lk2_ref[...], axis=-1, keepdims=True)) + lam_init)
    o = acc1_ref[...] * (1.0 / l1_ref[...]) - lam * (acc2_ref[...] * (1.0 / l2_ref[...]))
    o_ref[0] = ((_rms(o, DIFF_V) * g_ref[...]) * (1.0 - lam_init)).astype(o_ref.dtype)


def _diff_attn(sc, qd, kd, vd, lq1, lk1, lq2, lk2, gout, *, blk):
    B, S, _ = qd.shape
    est = 2 * (2 * blk * 128 * 2 + 2 * S * 128 * 2) + 6 * blk * 128 * 4 + 8 * blk * blk * 4

    def vec(a):
        return pl.BlockSpec(a.shape, lambda b, h, i: (0, 0))

    return pl.pallas_call(
        functools.partial(_diff_attn_body, blk=blk),
        grid=(B, DIFF_HEADS, S // blk),
        in_specs=[pl.BlockSpec(memory_space=pltpu.SMEM),
                  pl.BlockSpec((1, blk, 128), lambda b, h, i: (b, i, h)),
                  pl.BlockSpec((1, S, 128), lambda b, h, i: (b, 0, h)),
                  pl.BlockSpec((1, S, 128), lambda b, h, i: (b, 0, h)),
                  vec(lq1), vec(lk1), vec(lq2), vec(lk2), vec(gout)],
        out_specs=pl.BlockSpec((1, blk, 128), lambda b, h, i: (b, i, h)),
        out_shape=jax.ShapeDtypeStruct((B, S, DIFF_HEADS * DIFF_V), BF16),
        scratch_shapes=[pltpu.VMEM((blk, 1), F32), pltpu.VMEM((blk, 1), F32), pltpu.VMEM((blk, 128), F32),
                        pltpu.VMEM((blk, 1), F32), pltpu.VMEM((blk, 1), F32), pltpu.VMEM((blk, 128), F32)],
        compiler_params=pltpu.CompilerParams(
            dimension_semantics=("arbitrary", "arbitrary", "arbitrary"), vmem_limit_bytes=_vmem_limit(est)),
        name="diff_attn",
    )(sc, qd, kd, vd, lq1, lk1, lq2, lk2, gout)


def _dil_attn_body(sc_ref, q_ref, k_ref, v_ref, o_ref, op_ref, lse_ref, *, S):
    h = pl.program_id(1)
    slope = sc_ref[h]
    blk = DIL_BLOCK
    ii = lax.broadcasted_iota(jnp.int32, (blk, blk), 0)
    jj = lax.broadcasted_iota(jnp.int32, (blk, blk), 1)

    for p, (window, r) in enumerate(DIL_PATTERNS):
        assert window // r == blk
        nb = S // (r * blk)
        sr = slope * float(r)
        bias_prev = jnp.where(jj >= ii, -sr * (ii + blk - jj).astype(F32), NEG_INF)
        bias_cur = jnp.where(jj <= ii, -sr * (ii - jj).astype(F32), NEG_INF)

        def rows_of(start, r=r):
            return pl.ds(start, blk, stride=r) if r > 1 else pl.ds(start, blk)

        def body(idx, carry, r=r, nb=nb, p=p, bias_prev=bias_prev, bias_cur=bias_cur, rows_of=rows_of):
            cls = idx // nb
            n = idx % nb
            cur = rows_of(cls + n * (blk * r))
            prev = rows_of(cls + jnp.maximum(n - 1, 0) * (blk * r))
            q = q_ref[0, cur, :].astype(BF16)
            s_c = _dot_nt(q, k_ref[0, cur, :].astype(BF16)) + bias_cur
            s_p = _dot_nt(q, k_ref[0, prev, :].astype(BF16)) + bias_prev
            s_p = jnp.where(n >= 1, s_p, NEG_INF)
            m = jnp.maximum(jnp.max(s_c, axis=-1, keepdims=True), jnp.max(s_p, axis=-1, keepdims=True))
            e_c = jnp.exp(s_c - m)
            e_p = jnp.exp(s_p - m)
            l = jnp.sum(e_c, axis=-1, keepdims=True) + jnp.sum(e_p, axis=-1, keepdims=True)
            o = _dot(e_c.astype(BF16), v_ref[0, cur, :].astype(BF16)) + _dot(e_p.astype(BF16), v_ref[0, prev, :].astype(BF16))
            op_ref[p, cur, :] = o * (1.0 / l)
            lse_ref[p, cur, :] = jnp.broadcast_to(m + jnp.log(l), (blk, V7X_LANES))
            return carry

        lax.fori_loop(0, r * nb, body, 0)

    chunk = 256

    def combine(ci, carry):
        rows = pl.ds(pl.multiple_of(ci * chunk, chunk), chunk)
        l0 = lse_ref[0, rows, :]
        l1 = lse_ref[1, rows, :]
        l2 = lse_ref[2, rows, :]
        mm = jnp.maximum(jnp.maximum(l0, l1), l2)
        w0 = jnp.exp(l0 - mm)
        w1 = jnp.exp(l1 - mm)
        w2 = jnp.exp(l2 - mm)
        num = w0 * op_ref[0, rows, :] + w1 * op_ref[1, rows, :] + w2 * op_ref[2, rows, :]
        o_ref[0, rows, :] = (num * (1.0 / (w0 + w1 + w2))).astype(o_ref.dtype)
        return carry

    lax.fori_loop(0, S // chunk, combine, 0)


def _dil_attn(sc, ql, kl, vl):
    B, S, _ = ql.shape
    assert S % (DIL_BLOCK * max(r for _, r in DIL_PATTERNS)) == 0
    assert len(DIL_PATTERNS) == 3
    est = 2 * (3 * S * 128 * 4 + S * 128 * 2) + 6 * S * 128 * 4
    return pl.pallas_call(
        functools.partial(_dil_attn_body, S=S),
        grid=(B, DIL_HEADS),
        in_specs=[pl.BlockSpec(memory_space=pltpu.SMEM),
                  pl.BlockSpec((1, S, 128), lambda b, h: (b, 0, h)),
                  pl.BlockSpec((1, S, 128), lambda b, h: (b, 0, h)),
                  pl.BlockSpec((1, S, 128), lambda b, h: (b, 0, h))],
        out_specs=pl.BlockSpec((1, S, 128), lambda b, h: (b, 0, h)),
        out_shape=jax.ShapeDtypeStruct((B, S, DIL_HEADS * DIL_D), BF16),
        scratch_shapes=[pltpu.VMEM((3, S, 128), F32), pltpu.VMEM((3, S, 128), F32)],
        compiler_params=pltpu.CompilerParams(
            dimension_semantics=("arbitrary", "arbitrary"), vmem_limit_bytes=_vmem_limit(est)),
        name="dil_attn",
    )(sc, ql, kl, vl)


def _merge_body(x_ref, gz_ref, gb_ref, oa_ref, ob_ref, oc_ref, wa_ref, wb_ref, wc_ref, wo_ref, o_ref):
    D = x_ref.shape[-1]

    def gate(b):
        return jax.nn.sigmoid(gz_ref[:, b * D:(b + 1) * D] + gb_ref[:, b * D:(b + 1) * D])

    merged = (gate(0) * _dot(oa_ref[...], wa_ref[...])
              + gate(1) * _dot(ob_ref[...], wb_ref[...])
              + gate(2) * _dot(oc_ref[...], wc_ref[...]))
    o_ref[...] = x_ref[...] + _dot(merged.astype(BF16), wo_ref[...])


def _merge(x, z, gate_bias, oa, ob, oc, wa, wb, wc, wo, *, tm):
    T, D = x.shape
    hw = oa.shape[1]

    def resident(a):
        return pl.BlockSpec(a.shape, lambda i: (0, 0), pipeline_mode=pl.Buffered(1))

    def rows(width):
        return pl.BlockSpec((tm, width), lambda i: (i, 0))

    est = (2 * tm * D * 4 * 2 + 2 * tm * N_BRANCH * D * 4 + 2 * 3 * tm * hw * 2
           + (3 * hw * D + D * D) * 2 + N_BRANCH * D * 4 + 4 * tm * D * 4)
    return pl.pallas_call(
        _merge_body,
        grid=(T // tm,),
        in_specs=[rows(D), rows(N_BRANCH * D), resident(gate_bias), rows(hw), rows(hw), rows(hw),
                  resident(wa), resident(wb), resident(wc), resident(wo)],
        out_specs=rows(D),
        out_shape=jax.ShapeDtypeStruct((T, D), F32),
        compiler_params=pltpu.CompilerParams(
            dimension_semantics=("arbitrary",), vmem_limit_bytes=_vmem_limit(est)),
        name="merge",
    )(x, z, gate_bias, oa, ob, oc, wa, wb, wc, wo)


def _ffn_up_body(x_ref, g_ref, wg_ref, wu_ref, cwg_ref, cwu_ref, cbg_ref, cbu_ref, o_ref, h_ref, *, tm, tiles_per_seq):
    i = pl.program_id(0)

    @pl.when(pl.program_id(1) == 0)
    def _():
        @pl.when(i % tiles_per_seq == 0)
        def _():
            h_ref[0:HALO, :] = jnp.zeros((HALO, h_ref.shape[1]), BF16)

        @pl.when(i % tiles_per_seq != 0)
        def _():
            h_ref[0:HALO, :] = h_ref[tm:tm + HALO, :]

        x = x_ref[...]
        h_ref[HALO:HALO + tm, :] = (_rms(x, x.shape[-1]) * g_ref[...]).astype(BF16)

    h = h_ref[...]

    def conv(u, cw_ref, cb_ref):
        return (u[HALO:HALO + tm] * cw_ref[2:3, :] + u[HALO - 1:HALO - 1 + tm] * cw_ref[1:2, :]
                + u[HALO - 2:HALO - 2 + tm] * cw_ref[0:1, :] + cb_ref[...])

    gate = conv(_dot(h, wg_ref[...]), cwg_ref, cbg_ref)
    up = conv(_dot(h, wu_ref[...]), cwu_ref, cbu_ref)
    o_ref[...] = (gate * jax.nn.sigmoid(gate) * up).astype(o_ref.dtype)


def _ffn_up(x, g, w_up, conv_w, conv_b, *, S, tm, tn):
    T, K = x.shape
    F = w_up.shape[1] // 2
    nj = F // tn
    est = 2 * tm * K * 4 + (tm + HALO) * K * 2 + 2 * 2 * K * tn * 2 + 2 * tm * tn * 2 + 6 * (tm + HALO) * tn * 4
    return pl.pallas_call(
        functools.partial(_ffn_up_body, tm=tm, tiles_per_seq=S // tm),
        grid=(T // tm, nj),
        in_specs=[pl.BlockSpec((tm, K), lambda i, j: (i, 0)),
                  pl.BlockSpec((1, K), lambda i, j: (0, 0)),
                  pl.BlockSpec((K, tn), lambda i, j: (0, j)),
                  pl.BlockSpec((K, tn), lambda i, j: (0, j + nj)),
                  pl.BlockSpec((CONV_W, tn), lambda i, j: (0, j)),
                  pl.BlockSpec((CONV_W, tn), lambda i, j: (0, j + nj)),
                  pl.BlockSpec((1, tn), lambda i, j: (0, j)),
                  pl.BlockSpec((1, tn), lambda i, j: (0, j + nj))],
        out_specs=pl.BlockSpec((tm, tn), lambda i, j: (i, j)),
        out_shape=jax.ShapeDtypeStruct((T, F), BF16),
        scratch_shapes=[pltpu.VMEM((tm + HALO, K), BF16)],
        compiler_params=pltpu.CompilerParams(
            dimension_semantics=("arbitrary", "arbitrary"), vmem_limit_bytes=_vmem_limit(est)),
        name="ffn_up",
    )(x, g, w_up, w_up, conv_w, conv_w, conv_b, conv_b)


def _ffn_down_body(x_ref, a_ref, w_ref, o_ref):
    o_ref[...] = x_ref[...] + _dot(a_ref[...], w_ref[...])


def _ffn_down(x, act, w, *, tm, tn):
    T, D = x.shape
    F = act.shape[1]
    est = 2 * (2 * tm * tn * 4 + tm * F * 2 + F * tn * 2)
    return pl.pallas_call(
        _ffn_down_body,
        grid=(T // tm, D // tn),
        in_specs=[pl.BlockSpec((tm, tn), lambda i, j: (i, j)),
                  pl.BlockSpec((tm, F), lambda i, j: (i, 0)),
                  pl.BlockSpec((F, tn), lambda i, j: (0, j))],
        out_specs=pl.BlockSpec((tm, tn), lambda i, j: (i, j)),
        out_shape=jax.ShapeDtypeStruct((T, D), F32),
        compiler_params=pltpu.CompilerParams(
            dimension_semantics=("arbitrary", "arbitrary"), vmem_limit_bytes=_vmem_limit(est)),
        name="ffn_down",
    )(x, act, w)


def _rope_lanes(t):
    half = MLA_ROPE // 2
    z = jnp.zeros(t.shape[:-1] + (half,), t.dtype)
    return jnp.concatenate([t[..., :half], z, t[..., half:], z], axis=-1)


def _z_columns(D):
    hw = DIFF_HEADS * 128
    col = {"gz": 0}
    c = N_BRANCH * D
    for name in ("dq", "dk", "dv", "lq", "lk", "lv"):
        col[name] = c
        c += hw
    col["cq"] = c
    col["ckv"] = c + 512
    col["kr"] = c + 1024
    return col, c + 1024 + 128


def _w_in_layout(w, D, n_pad):
    hw = DIFF_HEADS * 128
    o_cq, o_ckv, o_kr = 0, 512, 1024
    o_d = 1024 + MLA_ROPE
    o_gz = o_d + 6 * hw
    parts = [w[:, o_gz:o_gz + N_BRANCH * D], w[:, o_d:o_gz], w[:, o_cq:o_ckv], w[:, o_ckv:o_kr],
             _rope_lanes(w[:, o_kr:o_d])]
    used = N_BRANCH * D + 6 * hw + 1024 + 128
    parts.append(jnp.zeros((w.shape[0], n_pad - used), w.dtype))
    return jnp.concatenate(parts, axis=1).astype(BF16)


def _w_uq_layout(w):
    r = w.reshape(w.shape[0], MLA_HEADS, MLA_NOPE + MLA_ROPE)
    out = jnp.concatenate([r[..., :MLA_NOPE], _rope_lanes(r[..., MLA_NOPE:])], axis=-1)
    return out.reshape(w.shape[0], MLA_HEADS * 2 * V7X_LANES).astype(BF16)


def _rope_tables(S):
    half = MLA_ROPE // 2
    inv = ROPE_THETA ** (-jnp.arange(half, dtype=F32) / half)
    ang = jnp.arange(S, dtype=F32)[:, None] * inv[None, :]
    cos, sin = jnp.cos(ang), jnp.sin(ang)
    z = jnp.zeros_like(cos)
    return jnp.concatenate([cos, z, cos, z], axis=-1), jnp.concatenate([-sin, z, sin, z], axis=-1)


def kernel(x, attn_norm_g, w_in, mla_q_lora_norm_g, mla_kv_lora_norm_g, mla_w_uq, mla_w_uk, mla_w_uv, mla_q_gain, mla_k_gain, diff_q_gain, diff_k_gain, diff_lam_q1, diff_lam_k1, diff_lam_q2, diff_lam_k2, diff_out_norm_g, dil_q_gain, dil_k_gain, gate_bias, w_br_mla, w_br_diff, w_br_dil, w_out, ffn_norm_g, ffn_w_up, ffn_conv_w, ffn_conv_b, ffn_w_down):
    B, S, D = x.shape
    depth = w_in.shape[0]
    T = B * S
    col, n_used = _z_columns(D)
    n_z = -(-n_used // 1024) * 1024
    cos_t, sin_t = _rope_tables(S)
    slopes = 2.0 ** (-8.0 * jnp.arange(1, DIFF_HEADS + DIL_HEADS + 1, dtype=F32) / (DIFF_HEADS + DIL_HEADS))
    s_mla = (MLA_NOPE + MLA_ROPE) ** -0.5
    row = lambda v: v.reshape(1, -1).astype(F32)

    xf = x.reshape(T, D)
    for l in range(depth):
        lam_init = 0.8 - 0.6 * math.exp(-0.3 * l)
        z = _norm_matmul(xf, row(attn_norm_g[l]), _w_in_layout(w_in[l], D, n_z), tm=512, tn=1024, out_dtype=F32)
        vecs = {
            "gql": row(mla_q_lora_norm_g[l]), "gkvl": row(mla_kv_lora_norm_g[l]),
            "gqn": row(mla_q_gain[l, :MLA_NOPE] * s_mla), "gqr": row(_rope_lanes(mla_q_gain[l, MLA_NOPE:]) * s_mla),
            "gkn": row(mla_k_gain[l, :MLA_NOPE]), "gkr": row(_rope_lanes(mla_k_gain[l, MLA_NOPE:])),
            "gdq": row(jnp.tile(diff_q_gain[l], 2) * DIFF_D ** -0.5), "gdk": row(jnp.tile(diff_k_gain[l], 2)),
            "glq": row(dil_q_gain[l] * DIL_D ** -0.5), "glk": row(dil_k_gain[l]),
        }
        wukv = jnp.concatenate([mla_w_uk[l], mla_w_uv[l]], axis=1).astype(BF16)
        qm, kn, kr, vm, qd, kd, vd, ql, kl, vl = _prep(
            z, cos_t, sin_t, vecs, _w_uq_layout(mla_w_uq[l]), wukv, S=S, tm=256, col=col)
        b3 = lambda a: a.reshape(B, S, a.shape[-1])
        o_mla = _mla_attn(b3(qm), b3(kn), b3(kr), b3(vm), blk=512)
        sc_diff = jnp.concatenate([slopes[0::2], jnp.full((2,), lam_init, F32)])
        o_diff = _diff_attn(sc_diff, b3(qd), b3(kd), b3(vd), row(diff_lam_q1[l]), row(diff_lam_k1[l]),
                            row(diff_lam_q2[l]), row(diff_lam_k2[l]), row(diff_out_norm_g[l]), blk=512)
        sc_dil = jnp.concatenate([slopes[1::2], jnp.zeros((2,), F32)])
        o_dil = _dil_attn(sc_dil, b3(ql), b3(kl), b3(vl))
        flat = lambda a: a.reshape(T, a.shape[-1])
        xf = _merge(xf, z, row(gate_bias[l]), flat(o_mla), flat(o_diff), flat(o_dil),
                    w_br_mla[l].astype(BF16), w_br_diff[l].astype(BF16), w_br_dil[l].astype(BF16),
                    w_out[l].astype(BF16), tm=256)
        act = _ffn_up(xf, row(ffn_norm_g[l]), ffn_w_up[l].astype(BF16), ffn_conv_w[l], row(ffn_conv_b[l]),
                      S=S, tm=512, tn=512)
        xf = _ffn_down(xf, act, ffn_w_down[l].astype(BF16), tm=512, tn=1024)
    return xf.reshape(B, S, D)
```

```python
import functools
import math

import jax
import jax.numpy as jnp
from jax import lax
from jax.experimental import pallas as pl
from jax.experimental.pallas import tpu as pltpu

F32 = jnp.float32
BF16 = jnp.bfloat16
EPS = 1e-6
NEG_INF = float("-inf")
LOG2E = math.log2(math.e)

V7X_LANES = 128
V7X_VMEM_BYTES = 64 * 1024 * 1024
BF16_ROWS_PER_VREG = 16

MLA_HEADS = 6
MLA_NOPE = 128
MLA_ROPE = 64
MLA_V = 128
ROPE_THETA = 10000.0
DIFF_HEADS = 6
DIFF_D = 64
DIFF_V = 128
DIL_HEADS = 6
DIL_D = 128
DIL_PATTERNS = ((128, 1), (512, 4), (2048, 16))
DIL_BLOCK = 128
DIL_UNROLL = 4
FLASH_CHUNK = 256
N_BRANCH = 3
CONV_W = 3
HALO = BF16_ROWS_PER_VREG
HW = 6 * 128


def _vmem_limit(block_bytes):
    return int(min(V7X_VMEM_BYTES - (4 << 20), block_bytes + (16 << 20)))


def _dot(a, b):
    return jnp.dot(a, b, preferred_element_type=F32)


def _dot_nt(a, b):
    return lax.dot_general(a, b, (((1,), (1,)), ((), ())), preferred_element_type=F32)


def _rms(x, n):
    return x * lax.rsqrt(jnp.sum(x * x, axis=-1, keepdims=True) * (1.0 / n) + EPS)


def _layer_vec(a, l):
    return pl.BlockSpec((None,) + a.shape[1:], lambda *_: (l, 0, 0))


def _norm_matmul_body(x_ref, g_ref, w_ref, o_ref, h_ref):
    @pl.when(pl.program_id(1) == 0)
    def _():
        x = x_ref[...]
        h_ref[...] = (_rms(x, x.shape[-1]) * g_ref[...]).astype(BF16)

    o_ref[...] = _dot(h_ref[...], w_ref[...]).astype(o_ref.dtype)


def _norm_matmul(x, g, w, l, *, tm, tn, out_dtype):
    T, K = x.shape
    N = w.shape[2]
    osz = jnp.dtype(out_dtype).itemsize
    est = 2 * tm * K * 4 + tm * K * 2 + 2 * K * tn * 2 + 2 * tm * tn * osz
    return pl.pallas_call(
        _norm_matmul_body,
        grid=(T // tm, N // tn),
        in_specs=[pl.BlockSpec((tm, K), lambda i, j: (i, 0)),
                  _layer_vec(g, l),
                  pl.BlockSpec((None, K, tn), lambda i, j: (l, 0, j))],
        out_specs=pl.BlockSpec((tm, tn), lambda i, j: (i, j)),
        out_shape=jax.ShapeDtypeStruct((T, N), out_dtype),
        scratch_shapes=[pltpu.VMEM((tm, K), BF16)],
        compiler_params=pltpu.CompilerParams(
            dimension_semantics=("arbitrary", "arbitrary"), vmem_limit_bytes=_vmem_limit(est)),
        name="norm_matmul",
    )(x, g, w)


def _rope(xb, c, s):
    return xb * c + pltpu.roll(xb, 64, 1) * s


def _prep_body(dq_ref, dk_ref, dv_ref, lq_ref, lk_ref, lv_ref, cq_ref, ckv_ref, kr_ref,
               cos_ref, sin_ref, gql_ref, gkvl_ref, wuq_ref, wuk_ref, wuvt_ref,
               gqn_ref, gqr_ref, gkn_ref, gkr_ref, gdq_ref, gdk_ref, glq_ref, glk_ref,
               qm_ref, km_ref, vmt_ref, qd_ref, kd_ref, vdt_ref, ql_ref, kl_ref, vl_ref):
    c = cos_ref[...]
    s = sin_ref[...]
    tm = c.shape[0]
    f32 = lambda ref: ref[...].astype(F32)

    cqn = (_rms(f32(cq_ref), cq_ref.shape[-1]) * gql_ref[...]).astype(BF16)
    q = _dot(cqn, wuq_ref[...])
    for h in range(MLA_HEADS):
        a = 2 * V7X_LANES * h
        qm_ref[:, a:a + 128] = (_rms(q[:, a:a + 128], MLA_NOPE) * gqn_ref[...]).astype(BF16)
        qr = _rms(q[:, a + 128:a + 256], MLA_ROPE) * gqr_ref[...]
        qm_ref[:, a + 128:a + 256] = _rope(qr, c, s).astype(BF16)
    ckvn = (_rms(f32(ckv_ref), ckv_ref.shape[-1]) * gkvl_ref[...]).astype(BF16)
    kn = _dot(ckvn, wuk_ref[...])
    kr = _rope(_rms(f32(kr_ref), MLA_ROPE) * gkr_ref[...], c, s).astype(BF16)
    for h in range(MLA_HEADS):
        a = V7X_LANES * h
        km_ref[:, 2 * a:2 * a + 128] = (_rms(kn[:, a:a + 128], MLA_NOPE) * gkn_ref[...]).astype(BF16)
        km_ref[:, 2 * a + 128:2 * a + 256] = kr
    vmt_ref[...] = _dot_nt(wuvt_ref[...], ckvn).astype(BF16)

    lo = lax.broadcasted_iota(jnp.int32, (tm, V7X_LANES), 1) < DIFF_D

    def half_rms(xh):
        sq = xh * xh
        s_lo = jnp.sum(jnp.where(lo, sq, 0.0), axis=-1, keepdims=True)
        s_hi = jnp.sum(jnp.where(lo, 0.0, sq), axis=-1, keepdims=True)
        inv = jnp.where(lo, lax.rsqrt(s_lo * (1.0 / DIFF_D) + EPS), lax.rsqrt(s_hi * (1.0 / DIFF_D) + EPS))
        return xh * inv

    for h in range(DIFF_HEADS):
        a = V7X_LANES * h
        qd_ref[:, a:a + 128] = (half_rms(dq_ref[:, a:a + 128].astype(F32)) * gdq_ref[...]).astype(BF16)
        kd_ref[:, a:a + 128] = (half_rms(dk_ref[:, a:a + 128].astype(F32)) * gdk_ref[...]).astype(BF16)
    vdt_ref[...] = f32(dv_ref).T.astype(BF16)

    for h in range(DIL_HEADS):
        a = V7X_LANES * h
        ql_ref[:, a:a + 128] = _rms(lq_ref[:, a:a + 128].astype(F32), DIL_D) * glq_ref[...]
        kl_ref[:, a:a + 128] = _rms(lk_ref[:, a:a + 128].astype(F32), DIL_D) * glk_ref[...]
    vl_ref[...] = f32(lv_ref)


def _prep(z, cos_t, sin_t, vecs, wuq, wuk, wuvt, l, *, S, tm, col):
    T = z.shape[0]
    nseq = S // tm

    def zspec(width, start):
        assert start % width == 0
        return pl.BlockSpec((tm, width), lambda i, blk=start // width: (i, blk))

    def wspec(a):
        return pl.BlockSpec((None,) + a.shape[1:], lambda i: (l, 0, 0))

    in_specs = [zspec(HW, col["dq"]), zspec(HW, col["dk"]), zspec(HW, col["dv"]),
                zspec(HW, col["lq"]), zspec(HW, col["lk"]), zspec(HW, col["lv"]),
                zspec(512, col["cq"]), zspec(512, col["ckv"]), zspec(128, col["kr"]),
                pl.BlockSpec((tm, 128), lambda i: (i % nseq, 0)),
                pl.BlockSpec((tm, 128), lambda i: (i % nseq, 0))]
    names = ["gql", "gkvl"]
    names2 = ["gqn", "gqr", "gkn", "gkr", "gdq", "gdk", "glq", "glk"]
    in_specs += [_layer_vec(vecs[n], l) for n in names] + [wspec(wuq), wspec(wuk), wspec(wuvt)]
    in_specs += [_layer_vec(vecs[n], l) for n in names2]

    def ospec(width):
        return pl.BlockSpec((tm, width), lambda i: (i, 0))

    tspec = pl.BlockSpec((HW, tm), lambda i: (0, i))
    out_shape = [jax.ShapeDtypeStruct((T, 2 * HW), BF16),
                 jax.ShapeDtypeStruct((T, 2 * HW), BF16),
                 jax.ShapeDtypeStruct((HW, T), BF16),
                 jax.ShapeDtypeStruct((T, HW), BF16),
                 jax.ShapeDtypeStruct((T, HW), BF16),
                 jax.ShapeDtypeStruct((HW, T), BF16),
                 jax.ShapeDtypeStruct((T, HW), F32),
                 jax.ShapeDtypeStruct((T, HW), F32),
                 jax.ShapeDtypeStruct((T, HW), F32)]
    out_specs = [ospec(2 * HW), ospec(2 * HW), tspec, ospec(HW), ospec(HW), tspec, ospec(HW), ospec(HW), ospec(HW)]
    est = 2 * tm * (6 * HW + 1024 + 128 + 256) * 4 + 2 * 2 * 512 * 1536 * 2 + 2 * tm * (12 * HW) * 4
    return pl.pallas_call(
        _prep_body,
        grid=(T // tm,),
        in_specs=in_specs,
        out_specs=out_specs,
        out_shape=out_shape,
        compiler_params=pltpu.CompilerParams(
            dimension_semantics=("arbitrary",), vmem_limit_bytes=_vmem_limit(est)),
        name="prep",
    )(z, z, z, z, z, z, z, z, z, cos_t, sin_t, vecs["gql"], vecs["gkvl"], wuq, wuk, wuvt,
      *[vecs[n] for n in names2])


def _online_update_t(m_ref, l_ref, acc_ref, t, shift, vt):
    m_prev = m_ref[...]
    m_new = jnp.maximum(m_prev, jnp.max(t, axis=0, keepdims=True) + shift)
    alpha = jnp.exp2(m_prev - m_new)
    p = jnp.exp2(t + (shift - m_new))
    l_ref[...] = alpha * l_ref[...] + jnp.sum(p, axis=0, keepdims=True)
    acc_ref[...] = alpha * acc_ref[...] + _dot(vt, p.astype(BF16))
    m_ref[...] = m_new


def _init_stats(m_ref, l_ref, acc_ref):
    m_ref[...] = jnp.full(m_ref.shape, NEG_INF, F32)
    l_ref[...] = jnp.zeros(l_ref.shape, F32)
    acc_ref[...] = jnp.zeros(acc_ref.shape, F32)


def _causal_sweep(qi, scores, consume, t0_ref, t1_ref):
    t0_ref[...] = scores(0)

    def pair(i, carry):
        t1_ref[...] = scores(2 * i + 1)
        consume(t0_ref, 2 * i, False)
        t0_ref[...] = scores(2 * i + 2)
        consume(t1_ref, 2 * i + 1, False)
        return carry

    lax.fori_loop(0, qi // 2, pair, 0)

    @pl.when(qi % 2 == 0)
    def _():
        consume(t0_ref, qi, True)

    @pl.when(qi % 2 == 1)
    def _():
        t1_ref[...] = scores(qi)
        consume(t0_ref, qi - 1, False)
        consume(t1_ref, qi, True)


def _mla_attn_body(q_ref, k_ref, vt_ref, o_ref, m_ref, l_ref, acc_ref, t0_ref, t1_ref, *, blk):
    qi = pl.program_id(2)
    _init_stats(m_ref, l_ref, acc_ref)

    def keys(kj):
        return pl.ds(pl.multiple_of(kj * blk, blk), blk)

    def scores(kj):
        return _dot_nt(k_ref[0, keys(kj), :], q_ref[0])

    def consume(t_ref, kj, masked):
        t = t_ref[...]
        if masked:
            key = lax.broadcasted_iota(jnp.int32, t.shape, 0)
            qry = lax.broadcasted_iota(jnp.int32, t.shape, 1)
            t = jnp.where(key <= qry, t, NEG_INF)
        _online_update_t(m_ref, l_ref, acc_ref, t, 0.0, vt_ref[:, keys(kj)])

    _causal_sweep(qi, scores, consume, t0_ref, t1_ref)
    o_ref[0] = (acc_ref[...] * (1.0 / l_ref[...])).T.astype(o_ref.dtype)


def _mla_attn(qm, km, vmt, *, blk):
    B, S, _ = qm.shape
    est = 2 * (blk * 256 * 2 + S * 256 * 2 + S * 128 * 2 + blk * 128 * 2) + 3 * blk * 128 * 4 + 4 * blk * blk * 4
    return pl.pallas_call(
        functools.partial(_mla_attn_body, blk=blk),
        grid=(B, MLA_HEADS, S // blk),
        in_specs=[pl.BlockSpec((1, blk, 256), lambda b, h, i: (b, i, h)),
                  pl.BlockSpec((1, S, 256), lambda b, h, i: (b, 0, h)),
                  pl.BlockSpec((128, S), lambda b, h, i: (h, b))],
        out_specs=pl.BlockSpec((1, blk, 128), lambda b, h, i: (b, i, h)),
        out_shape=jax.ShapeDtypeStruct((B, S, MLA_HEADS * MLA_V), BF16),
        scratch_shapes=[pltpu.VMEM((1, blk), F32), pltpu.VMEM((1, blk), F32), pltpu.VMEM((128, blk), F32),
                        pltpu.VMEM((blk, blk), F32), pltpu.VMEM((blk, blk), F32)],
        compiler_params=pltpu.CompilerParams(
            dimension_semantics=("arbitrary", "arbitrary", "arbitrary"), vmem_limit_bytes=_vmem_limit(est)),
        name="mla_attn",
    )(qm, km, vmt)


def _diff_attn_body(sc_ref, q_ref, k_ref, vt_ref, lq1_ref, lk1_ref, lq2_ref, lk2_ref, g_ref, o_ref,
                    m_ref, l_ref, acc_ref, qq_ref, t0_ref, t1_ref, *, blk):
    h = pl.program_id(1)
    qi = pl.program_id(2)
    slope = sc_ref[h]
    lam_init = sc_ref[DIFF_HEADS]
    q = q_ref[0]
    lo = lax.broadcasted_iota(jnp.int32, q.shape, 1) < DIFF_D
    zero = jnp.zeros(q.shape, q.dtype)
    qq_ref[0:blk, :] = jnp.where(lo, q, zero)
    qq_ref[blk:2 * blk, :] = jnp.where(lo, zero, q)
    _init_stats(m_ref, l_ref, acc_ref)
    key = lax.broadcasted_iota(jnp.int32, (blk, 2 * blk), 0)
    col = lax.broadcasted_iota(jnp.int32, (blk, 2 * blk), 1)
    qry = jnp.where(col >= blk, col - blk, col)

    def keys(kj):
        return pl.ds(pl.multiple_of(kj * blk, blk), blk)

    def scores(kj):
        return _dot_nt(k_ref[0, keys(kj), :], qq_ref[...]) + slope * key.astype(F32)

    def consume(t_ref, kj, masked):
        t = t_ref[...]
        if masked:
            t = jnp.where(key <= qry, t, NEG_INF)
        shift = slope * ((kj - qi) * blk).astype(F32)
        _online_update_t(m_ref, l_ref, acc_ref, t, shift, vt_ref[:, keys(kj)])

    _causal_sweep(qi, scores, consume, t0_ref, t1_ref)

    lam = (jnp.exp(jnp.sum(lq1_ref[...] * lk1_ref[...], axis=-1, keepdims=True))
           - jnp.exp(jnp.sum(lq2_ref[...] * lk2_ref[...], axis=-1, keepdims=True)) + lam_init)
    on = acc_ref[...] * (1.0 / l_ref[...])
    o = on[:, :blk] - lam * on[:, blk:]
    o = o * lax.rsqrt(jnp.sum(o * o, axis=0, keepdims=True) * (1.0 / DIFF_V) + EPS)
    o_ref[0] = ((o * g_ref[...]) * (1.0 - lam_init)).T.astype(o_ref.dtype)


def _diff_attn(sc, qd, kd, vdt, lq1, lk1, lq2, lk2, gout, l, *, blk):
    B, S, _ = qd.shape
    est = 2 * (2 * blk * 128 * 2 + 2 * S * 128 * 2) + 2 * blk * 128 * 4 + 8 * blk * 2 * blk * 4
    return pl.pallas_call(
        functools.partial(_diff_attn_body, blk=blk),
        grid=(B, DIFF_HEADS, S // blk),
        in_specs=[pl.BlockSpec(memory_space=pltpu.SMEM),
                  pl.BlockSpec((1, blk, 128), lambda b, h, i: (b, i, h)),
                  pl.BlockSpec((1, S, 128), lambda b, h, i: (b, 0, h)),
                  pl.BlockSpec((128, S), lambda b, h, i: (h, b)),
                  _layer_vec(lq1, l), _layer_vec(lk1, l), _layer_vec(lq2, l), _layer_vec(lk2, l),
                  _layer_vec(gout, l)],
        out_specs=pl.BlockSpec((1, blk, 128), lambda b, h, i: (b, i, h)),
        out_shape=jax.ShapeDtypeStruct((B, S, DIFF_HEADS * DIFF_V), BF16),
        scratch_shapes=[pltpu.VMEM((1, 2 * blk), F32), pltpu.VMEM((1, 2 * blk), F32),
                        pltpu.VMEM((128, 2 * blk), F32), pltpu.VMEM((2 * blk, 128), BF16),
                        pltpu.VMEM((blk, 2 * blk), F32), pltpu.VMEM((blk, 2 * blk), F32)],
        compiler_params=pltpu.CompilerParams(
            dimension_semantics=("arbitrary", "arbitrary", "arbitrary"), vmem_limit_bytes=_vmem_limit(est)),
        name="diff_attn",
    )(sc, qd, kd, vdt, lq1, lk1, lq2, lk2, gout)


def _dil_attn_body(sc_ref, q_ref, k_ref, v_ref, o_ref, qs_ref, ks_ref, vs_ref, op_ref, lse_ref, *, S):
    h = pl.program_id(1)
    slope = sc_ref[h]
    blk = DIL_BLOCK
    ii = lax.broadcasted_iota(jnp.int32, (blk, 2 * blk), 0)
    jj = lax.broadcasted_iota(jnp.int32, (blk, 2 * blk), 1)
    back = ii + blk - jj
    in_band = (back >= 0) & (back <= blk)
    ones = jnp.ones((blk, V7X_LANES), BF16)

    for p, (window, r) in enumerate(DIL_PATTERNS):
        assert window // r == blk
        L = S // r
        nb = L // blk
        Lp = L + blk
        bias = jnp.where(in_band, (-slope * float(r)) * back.astype(F32), NEG_INF)

        def src_rows(start, r=r):
            return pl.ds(start, blk, stride=r) if r > 1 else pl.ds(start, blk)

        def zero_pad(c, carry, Lp=Lp):
            pad = pl.ds(pl.multiple_of(c * Lp, blk), blk)
            ks_ref[pad, :] = jnp.zeros((blk, V7X_LANES), BF16)
            vs_ref[pad, :] = jnp.zeros((blk, 2 * V7X_LANES), BF16)
            return carry

        lax.fori_loop(0, r, zero_pad, 0)

        def gather(idx, carry, r=r, nb=nb, L=L, Lp=Lp, src_rows=src_rows):
            c = idx // nb
            n = idx % nb
            src = src_rows(c + n * (blk * r))
            dst = pl.ds(pl.multiple_of(c * Lp + (n + 1) * blk, blk), blk)
            qs_ref[pl.ds(pl.multiple_of(c * L + n * blk, blk), blk), :] = q_ref[0, src, :].astype(BF16)
            ks_ref[dst, :] = k_ref[0, src, :].astype(BF16)
            vs_ref[dst, :] = jnp.concatenate([v_ref[0, src, :].astype(BF16), ones], axis=1)
            return carry

        lax.fori_loop(0, r * nb, gather, 0)

        def blocks(it, carry, r=r, nb=nb, L=L, Lp=Lp, p=p, bias=bias, src_rows=src_rows):
            for u in range(DIL_UNROLL):
                idx = it * DIL_UNROLL + u
                c = idx // nb
                n = idx % nb
                q = qs_ref[pl.ds(pl.multiple_of(c * L + n * blk, blk), blk), :]
                keys = pl.ds(pl.multiple_of(c * Lp + n * blk, blk), 2 * blk)
                s = _dot_nt(q, ks_ref[keys, :]) + bias
                s = jnp.where((n >= 1) | (jj >= blk), s, NEG_INF)
                m = jnp.max(s, axis=-1, keepdims=True)
                e = jnp.exp2(s - m)
                ov = _dot(e.astype(BF16), vs_ref[keys, :])
                den = ov[:, V7X_LANES:]
                dst = src_rows(c + n * (blk * r))
                op_ref[p, dst, :] = ov[:, :V7X_LANES] * (1.0 / den)
                lse_ref[p, dst, :] = m + jnp.log2(den)
            return carry

        lax.fori_loop(0, (r * nb) // DIL_UNROLL, blocks, 0)

    chunk = 256

    def combine(ci, carry):
        rows = pl.ds(pl.multiple_of(ci * chunk, chunk), chunk)
        l0 = lse_ref[0, rows, :]
        l1 = lse_ref[1, rows, :]
        l2 = lse_ref[2, rows, :]
        mm = jnp.maximum(jnp.maximum(l0, l1), l2)
        w0 = jnp.exp2(l0 - mm)
        w1 = jnp.exp2(l1 - mm)
        w2 = jnp.exp2(l2 - mm)
        num = w0 * op_ref[0, rows, :] + w1 * op_ref[1, rows, :] + w2 * op_ref[2, rows, :]
        o_ref[0, rows, :] = (num * (1.0 / (w0 + w1 + w2))).astype(o_ref.dtype)
        return carry

    lax.fori_loop(0, S // chunk, combine, 0)


def _dil_attn(sc, ql, kl, vl):
    B, S, _ = ql.shape
    r_max = max(r for _, r in DIL_PATTERNS)
    assert S % (DIL_BLOCK * r_max) == 0 and (S // DIL_BLOCK) % DIL_UNROLL == 0
    assert len(DIL_PATTERNS) == 3
    pad_rows = S + DIL_BLOCK * r_max
    est = (2 * (3 * S * 128 * 4 + S * 128 * 2) + 6 * S * 128 * 4
           + S * 128 * 2 + pad_rows * 128 * 2 + pad_rows * 256 * 2)
    return pl.pallas_call(
        functools.partial(_dil_attn_body, S=S),
        grid=(B, DIL_HEADS),
        in_specs=[pl.BlockSpec(memory_space=pltpu.SMEM),
                  pl.BlockSpec((1, S, 128), lambda b, h: (b, 0, h)),
                  pl.BlockSpec((1, S, 128), lambda b, h: (b, 0, h)),
                  pl.BlockSpec((1, S, 128), lambda b, h: (b, 0, h))],
        out_specs=pl.BlockSpec((1, S, 128), lambda b, h: (b, 0, h)),
        out_shape=jax.ShapeDtypeStruct((B, S, DIL_HEADS * DIL_D), BF16),
        scratch_shapes=[pltpu.VMEM((S, 128), BF16), pltpu.VMEM((pad_rows, 128), BF16),
                        pltpu.VMEM((pad_rows, 256), BF16),
                        pltpu.VMEM((3, S, 128), F32), pltpu.VMEM((3, S, 128), F32)],
        compiler_params=pltpu.CompilerParams(
            dimension_semantics=("arbitrary", "arbitrary"), vmem_limit_bytes=_vmem_limit(est)),
        name="dil_attn",
    )(sc, ql, kl, vl)


def _merge_body(x_ref, gz_ref, gb_ref, oa_ref, ob_ref, oc_ref, wa_ref, wb_ref, wc_ref, wo_ref, o_ref):
    D = x_ref.shape[-1]

    def gate(b):
        return jax.nn.sigmoid(gz_ref[:, b * D:(b + 1) * D].astype(F32) + gb_ref[:, b * D:(b + 1) * D])

    merged = (gate(0) * _dot(oa_ref[...], wa_ref[...])
              + gate(1) * _dot(ob_ref[...], wb_ref[...])
              + gate(2) * _dot(oc_ref[...], wc_ref[...]))
    o_ref[...] = x_ref[...] + _dot(merged.astype(BF16), wo_ref[...])


def _merge(x, z, gate_bias, oa, ob, oc, wa, wb, wc, wo, l, *, tm):
    T, D = x.shape

    def resident(a):
        return pl.BlockSpec((None,) + a.shape[1:], lambda i: (l, 0, 0), pipeline_mode=pl.Buffered(1))

    def rows(width):
        return pl.BlockSpec((tm, width), lambda i: (i, 0))

    zsz = jnp.dtype(z.dtype).itemsize
    est = (2 * tm * D * 4 * 2 + 2 * tm * N_BRANCH * D * zsz + 2 * 3 * tm * HW * 2
           + (3 * HW * D + D * D) * 2 + N_BRANCH * D * 4 + 4 * tm * D * 4)
    return pl.pallas_call(
        _merge_body,
        grid=(T // tm,),
        in_specs=[rows(D), rows(N_BRANCH * D), resident(gate_bias), rows(HW), rows(HW), rows(HW),
                  resident(wa), resident(wb), resident(wc), resident(wo)],
        out_specs=rows(D),
        out_shape=jax.ShapeDtypeStruct((T, D), F32),
        compiler_params=pltpu.CompilerParams(
            dimension_semantics=("arbitrary",), vmem_limit_bytes=_vmem_limit(est)),
        name="merge",
    )(x, z, gate_bias, oa, ob, oc, wa, wb, wc, wo)


def _ffn_up_body(x_ref, g_ref, wg_ref, wu_ref, cwg_ref, cwu_ref, cbg_ref, cbu_ref, o_ref, h_ref, *, tm, tiles_per_seq):
    i = pl.program_id(0)

    @pl.when(pl.program_id(1) == 0)
    def _():
        @pl.when(i % tiles_per_seq == 0)
        def _():
            h_ref[0:HALO, :] = jnp.zeros((HALO, h_ref.shape[1]), BF16)

        @pl.when(i % tiles_per_seq != 0)
        def _():
            h_ref[0:HALO, :] = h_ref[tm:tm + HALO, :]

        x = x_ref[...]
        h_ref[HALO:HALO + tm, :] = (_rms(x, x.shape[-1]) * g_ref[...]).astype(BF16)

    h = h_ref[...]

    def conv(u, cw_ref, cb_ref):
        return (u[HALO:HALO + tm] * cw_ref[2:3, :] + u[HALO - 1:HALO - 1 + tm] * cw_ref[1:2, :]
                + u[HALO - 2:HALO - 2 + tm] * cw_ref[0:1, :] + cb_ref[...])

    gate = conv(_dot(h, wg_ref[...]), cwg_ref, cbg_ref)
    up = conv(_dot(h, wu_ref[...]), cwu_ref, cbu_ref)
    o_ref[...] = (gate * jax.nn.sigmoid(gate) * up).astype(o_ref.dtype)


def _ffn_up(x, g, w_up, conv_w, conv_b, l, *, S, tm, tn):
    T, K = x.shape
    F = w_up.shape[2] // 2
    nj = F // tn
    est = 2 * tm * K * 4 + (tm + HALO) * K * 2 + 2 * 2 * K * tn * 2 + 2 * tm * tn * 2 + 6 * (tm + HALO) * tn * 4
    return pl.pallas_call(
        functools.partial(_ffn_up_body, tm=tm, tiles_per_seq=S // tm),
        grid=(T // tm, nj),
        in_specs=[pl.BlockSpec((tm, K), lambda i, j: (i, 0)),
                  _layer_vec(g, l),
                  pl.BlockSpec((None, K, tn), lambda i, j: (l, 0, j)),
                  pl.BlockSpec((None, K, tn), lambda i, j: (l, 0, j + nj)),
                  pl.BlockSpec((None, CONV_W, tn), lambda i, j: (l, 0, j)),
                  pl.BlockSpec((None, CONV_W, tn), lambda i, j: (l, 0, j + nj)),
                  pl.BlockSpec((None, 1, tn), lambda i, j: (l, 0, j)),
                  pl.BlockSpec((None, 1, tn), lambda i, j: (l, 0, j + nj))],
        out_specs=pl.BlockSpec((tm, tn), lambda i, j: (i, j)),
        out_shape=jax.ShapeDtypeStruct((T, F), BF16),
        scratch_shapes=[pltpu.VMEM((tm + HALO, K), BF16)],
        compiler_params=pltpu.CompilerParams(
            dimension_semantics=("arbitrary", "arbitrary"), vmem_limit_bytes=_vmem_limit(est)),
        name="ffn_up",
    )(x, g, w_up, w_up, conv_w, conv_w, conv_b, conv_b)


def _ffn_down_body(x_ref, a_ref, w_ref, o_ref):
    o_ref[...] = x_ref[...] + _dot(a_ref[...], w_ref[...])


def _ffn_down(x, act, w, l, *, tm, tn):
    T, D = x.shape
    F = act.shape[1]
    est = 2 * (2 * tm * tn * 4 + tm * F * 2 + F * tn * 2)
    return pl.pallas_call(
        _ffn_down_body,
        grid=(T // tm, D // tn),
        in_specs=[pl.BlockSpec((tm, tn), lambda i, j: (i, j)),
                  pl.BlockSpec((tm, F), lambda i, j: (i, 0)),
                  pl.BlockSpec((None, F, tn), lambda i, j: (l, 0, j))],
        out_specs=pl.BlockSpec((tm, tn), lambda i, j: (i, j)),
        out_shape=jax.ShapeDtypeStruct((T, D), F32),
        compiler_params=pltpu.CompilerParams(
            dimension_semantics=("arbitrary", "arbitrary"), vmem_limit_bytes=_vmem_limit(est)),
        name="ffn_down",
    )(x, act, w)


def _rope_lanes(t):
    half = MLA_ROPE // 2
    z = jnp.zeros(t.shape[:-1] + (half,), t.dtype)
    return jnp.concatenate([t[..., :half], z, t[..., half:], z], axis=-1)


def _z_columns(D):
    col = {"gz": 0}
    c = N_BRANCH * D
    for name in ("dq", "dk", "dv", "lq", "lk", "lv"):
        col[name] = c
        c += HW
    col["cq"] = c
    col["ckv"] = c + 512
    col["kr"] = c + 1024
    return col, c + 1024 + 128


def _w_in_layout(w, D, n_pad):
    o_cq, o_ckv, o_kr = 0, 512, 1024
    o_d = 1024 + MLA_ROPE
    o_gz = o_d + 6 * HW
    parts = [w[..., o_gz:o_gz + N_BRANCH * D], w[..., o_d:o_gz], w[..., o_cq:o_ckv], w[..., o_ckv:o_kr],
             _rope_lanes(w[..., o_kr:o_d])]
    used = N_BRANCH * D + 6 * HW + 1024 + 128
    parts.append(jnp.zeros(w.shape[:-1] + (n_pad - used,), w.dtype))
    return jnp.concatenate([p.astype(BF16) for p in parts], axis=-1)


def _w_uq_layout(w):
    r = w.reshape(w.shape[:-1] + (MLA_HEADS, MLA_NOPE + MLA_ROPE))
    out = jnp.concatenate([r[..., :MLA_NOPE], _rope_lanes(r[..., MLA_NOPE:])], axis=-1)
    return out.reshape(w.shape[:-1] + (MLA_HEADS * 2 * V7X_LANES,)).astype(BF16)


def _rope_tables(S):
    half = MLA_ROPE // 2
    inv = ROPE_THETA ** (-jnp.arange(half, dtype=F32) / half)
    ang = jnp.arange(S, dtype=F32)[:, None] * inv[None, :]
    cos, sin = jnp.cos(ang), jnp.sin(ang)
    z = jnp.zeros_like(cos)
    return jnp.concatenate([cos, z, cos, z], axis=-1), jnp.concatenate([-sin, z, sin, z], axis=-1)


def kernel(x, attn_norm_g, w_in, mla_q_lora_norm_g, mla_kv_lora_norm_g, mla_w_uq, mla_w_uk, mla_w_uv, mla_q_gain, mla_k_gain, diff_q_gain, diff_k_gain, diff_lam_q1, diff_lam_k1, diff_lam_q2, diff_lam_k2, diff_out_norm_g, dil_q_gain, dil_k_gain, gate_bias, w_br_mla, w_br_diff, w_br_dil, w_out, ffn_norm_g, ffn_w_up, ffn_conv_w, ffn_conv_b, ffn_w_down):
    B, S, D = x.shape
    depth = w_in.shape[0]
    T = B * S
    col, n_used = _z_columns(D)
    n_z = -(-n_used // 1024) * 1024
    cos_t, sin_t = _rope_tables(S)
    slopes = LOG2E * 2.0 ** (-8.0 * jnp.arange(1, DIFF_HEADS + DIL_HEADS + 1, dtype=F32) / (DIFF_HEADS + DIL_HEADS))
    s_mla = LOG2E * (MLA_NOPE + MLA_ROPE) ** -0.5
    rows = lambda v: v.reshape(depth, 1, -1).astype(F32)

    w_in_p = _w_in_layout(w_in, D, n_z)
    w_uq_p = _w_uq_layout(mla_w_uq)
    w_uk_p = mla_w_uk.astype(BF16)
    w_uvt_p = jnp.swapaxes(mla_w_uv, 1, 2).astype(BF16)
    w_bra, w_brb, w_brc, w_out_p = (w.astype(BF16) for w in (w_br_mla, w_br_diff, w_br_dil, w_out))
    w_up_p, w_down_p = ffn_w_up.astype(BF16), ffn_w_down.astype(BF16)
    vecs = {
        "gql": rows(mla_q_lora_norm_g), "gkvl": rows(mla_kv_lora_norm_g),
        "gqn": rows(mla_q_gain[:, :MLA_NOPE] * s_mla), "gqr": rows(_rope_lanes(mla_q_gain[:, MLA_NOPE:]) * s_mla),
        "gkn": rows(mla_k_gain[:, :MLA_NOPE]), "gkr": rows(_rope_lanes(mla_k_gain[:, MLA_NOPE:])),
        "gdq": rows(jnp.tile(diff_q_gain, (1, 2)) * (LOG2E * DIFF_D ** -0.5)), "gdk": rows(jnp.tile(diff_k_gain, (1, 2))),
        "glq": rows(dil_q_gain * (LOG2E * DIL_D ** -0.5)), "glk": rows(dil_k_gain),
    }
    g_attn, g_ffn, g_bias, c_b = rows(attn_norm_g), rows(ffn_norm_g), rows(gate_bias), rows(ffn_conv_b)
    lam_q1, lam_k1, lam_q2, lam_k2 = rows(diff_lam_q1), rows(diff_lam_k1), rows(diff_lam_q2), rows(diff_lam_k2)
    g_dout = diff_out_norm_g.reshape(depth, DIFF_V, 1).astype(F32)
    sc_dil = jnp.concatenate([slopes[1::2], jnp.zeros((2,), F32)])

    xf = x.reshape(T, D)
    b3 = lambda a: a.reshape(B, S, a.shape[-1])
    flat = lambda a: a.reshape(T, a.shape[-1])
    for l in range(depth):
        lam_init = 0.8 - 0.6 * math.exp(-0.3 * l)
        z = _norm_matmul(xf, g_attn, w_in_p, l, tm=512, tn=1024, out_dtype=BF16)
        qm, km, vmt, qd, kd, vdt, ql, kl, vl = _prep(
            z, cos_t, sin_t, vecs, w_uq_p, w_uk_p, w_uvt_p, l, S=S, tm=256, col=col)
        o_mla = _mla_attn(b3(qm), b3(km), vmt, blk=512)
        sc_diff = jnp.concatenate([slopes[0::2], jnp.full((2,), lam_init, F32)])
        o_diff = _diff_attn(sc_diff, b3(qd), b3(kd), vdt, lam_q1, lam_k1, lam_q2, lam_k2, g_dout, l, blk=512)
        o_dil = _dil_attn(sc_dil, b3(ql), b3(kl), b3(vl))
        xf = _merge(xf, z, g_bias, flat(o_mla), flat(o_diff), flat(o_dil), w_bra, w_brb, w_brc, w_out_p, l, tm=256)
        act = _ffn_up(xf, g_ffn, w_up_p, ffn_conv_w, c_b, l, S=S, tm=512, tn=512)
        xf = _ffn_down(xf, act, w_down_p, l, tm=512, tn=1024)
    return xf.reshape(B, S, D)
```

```python
import functools
import math

import jax
import jax.numpy as jnp
from jax import lax
from jax.experimental import pallas as pl
from jax.experimental.pallas import tpu as pltpu

F32 = jnp.float32
BF16 = jnp.bfloat16
EPS = 1e-6
NEG_INF = float("-inf")
LOG2E = math.log2(math.e)

V7X_LANES = 128
V7X_VMEM_BYTES = 64 * 1024 * 1024
BF16_ROWS_PER_VREG = 16

MLA_HEADS = 6
MLA_NOPE = 128
MLA_ROPE = 64
MLA_V = 128
ROPE_THETA = 10000.0
DIFF_HEADS = 6
DIFF_D = 64
DIFF_V = 128
DIL_HEADS = 6
DIL_D = 128
DIL_PATTERNS = ((128, 1), (512, 4), (2048, 16))
DIL_BLOCK = 128
DIL_UNROLL = 4
N_BRANCH = 3
CONV_W = 3
HALO = BF16_ROWS_PER_VREG
HW = 6 * 128
VT_ROWS = DIFF_V + BF16_ROWS_PER_VREG


def _vmem_limit(block_bytes):
    return int(min(V7X_VMEM_BYTES - (4 << 20), block_bytes + (16 << 20)))


def _dot(a, b):
    return jnp.dot(a, b, preferred_element_type=F32)


def _dot_nt(a, b):
    return lax.dot_general(a, b, (((1,), (1,)), ((), ())), preferred_element_type=F32)


def _rms(x, n):
    return x * lax.rsqrt(jnp.sum(x * x, axis=-1, keepdims=True) * (1.0 / n) + EPS)


def _layer_vec(a, l):
    return pl.BlockSpec((None,) + a.shape[1:], lambda *_: (l, 0, 0))


def _norm_matmul_body(x_ref, g_ref, w_ref, o_ref, h_ref):
    @pl.when(pl.program_id(1) == 0)
    def _():
        x = x_ref[...]
        h_ref[...] = (_rms(x, x.shape[-1]) * g_ref[...]).astype(BF16)

    o_ref[...] = _dot(h_ref[...], w_ref[...]).astype(o_ref.dtype)


def _norm_matmul(x, g, w, l, *, tm, tn, out_dtype):
    T, K = x.shape
    N = w.shape[2]
    osz = jnp.dtype(out_dtype).itemsize
    est = 2 * tm * K * 4 + tm * K * 2 + 2 * K * tn * 2 + 2 * tm * tn * osz
    return pl.pallas_call(
        _norm_matmul_body,
        grid=(T // tm, N // tn),
        in_specs=[pl.BlockSpec((tm, K), lambda i, j: (i, 0)),
                  _layer_vec(g, l),
                  pl.BlockSpec((None, K, tn), lambda i, j: (l, 0, j))],
        out_specs=pl.BlockSpec((tm, tn), lambda i, j: (i, j)),
        out_shape=jax.ShapeDtypeStruct((T, N), out_dtype),
        scratch_shapes=[pltpu.VMEM((tm, K), BF16)],
        compiler_params=pltpu.CompilerParams(
            dimension_semantics=("arbitrary", "arbitrary"), vmem_limit_bytes=_vmem_limit(est)),
        name="norm_matmul",
    )(x, g, w)


def _rope(xb, c, s):
    return xb * c + pltpu.roll(xb, 64, 1) * s


def _prep_body(dq_ref, dk_ref, dv_ref, lq_ref, lk_ref, lv_ref, cq_ref, ckv_ref, kr_ref,
               cos_ref, sin_ref, gql_ref, gkvl_ref, wuq_ref, wuk_ref, wuvt_ref,
               gqn_ref, gqr_ref, gkn_ref, gkr_ref, gdq_ref, gdk_ref, glq_ref, glk_ref,
               qm_ref, km_ref, vmt_ref, qd_ref, kd_ref, vdt_ref, ql_ref, kl_ref, vl_ref):
    c = cos_ref[...]
    s = sin_ref[...]
    tm = c.shape[0]
    f32 = lambda ref: ref[...].astype(F32)

    cqn = (_rms(f32(cq_ref), cq_ref.shape[-1]) * gql_ref[...]).astype(BF16)
    q = _dot(cqn, wuq_ref[...])
    for h in range(MLA_HEADS):
        a = 2 * V7X_LANES * h
        qm_ref[:, a:a + 128] = (_rms(q[:, a:a + 128], MLA_NOPE) * gqn_ref[...]).astype(BF16)
        qr = _rms(q[:, a + 128:a + 256], MLA_ROPE) * gqr_ref[...]
        qm_ref[:, a + 128:a + 256] = _rope(qr, c, s).astype(BF16)
    ckvn = (_rms(f32(ckv_ref), ckv_ref.shape[-1]) * gkvl_ref[...]).astype(BF16)
    kn = _dot(ckvn, wuk_ref[...])
    kr = _rope(_rms(f32(kr_ref), MLA_ROPE) * gkr_ref[...], c, s).astype(BF16)
    for h in range(MLA_HEADS):
        a = V7X_LANES * h
        km_ref[:, 2 * a:2 * a + 128] = (_rms(kn[:, a:a + 128], MLA_NOPE) * gkn_ref[...]).astype(BF16)
        km_ref[:, 2 * a + 128:2 * a + 256] = kr
    vmt_ref[...] = _dot_nt(wuvt_ref[...], ckvn).astype(BF16)

    lo = lax.broadcasted_iota(jnp.int32, (tm, V7X_LANES), 1) < DIFF_D

    def half_rms(xh):
        sq = xh * xh
        s_lo = jnp.sum(jnp.where(lo, sq, 0.0), axis=-1, keepdims=True)
        s_hi = jnp.sum(jnp.where(lo, 0.0, sq), axis=-1, keepdims=True)
        inv = jnp.where(lo, lax.rsqrt(s_lo * (1.0 / DIFF_D) + EPS), lax.rsqrt(s_hi * (1.0 / DIFF_D) + EPS))
        return xh * inv

    for h in range(DIFF_HEADS):
        a = V7X_LANES * h
        qd_ref[:, a:a + 128] = (half_rms(dq_ref[:, a:a + 128].astype(F32)) * gdq_ref[...]).astype(BF16)
        kd_ref[:, a:a + 128] = (half_rms(dk_ref[:, a:a + 128].astype(F32)) * gdk_ref[...]).astype(BF16)
    vdt = f32(dv_ref).T.astype(BF16)
    for h in range(DIFF_HEADS):
        vdt_ref[VT_ROWS * h:VT_ROWS * h + DIFF_V, :] = vdt[DIFF_V * h:DIFF_V * (h + 1)]
        vdt_ref[VT_ROWS * h + DIFF_V:VT_ROWS * (h + 1), :] = jnp.ones((VT_ROWS - DIFF_V, tm), BF16)

    for h in range(DIL_HEADS):
        a = V7X_LANES * h
        ql_ref[:, a:a + 128] = _rms(lq_ref[:, a:a + 128].astype(F32), DIL_D) * glq_ref[...]
        kl_ref[:, a:a + 128] = _rms(lk_ref[:, a:a + 128].astype(F32), DIL_D) * glk_ref[...]
    vl_ref[...] = f32(lv_ref)


def _prep(z, cos_t, sin_t, vecs, wuq, wuk, wuvt, l, *, S, tm, col):
    T = z.shape[0]
    nseq = S // tm

    def zspec(width, start):
        assert start % width == 0
        return pl.BlockSpec((tm, width), lambda i, blk=start // width: (i, blk))

    def wspec(a):
        return pl.BlockSpec((None,) + a.shape[1:], lambda i: (l, 0, 0))

    in_specs = [zspec(HW, col["dq"]), zspec(HW, col["dk"]), zspec(HW, col["dv"]),
                zspec(HW, col["lq"]), zspec(HW, col["lk"]), zspec(HW, col["lv"]),
                zspec(512, col["cq"]), zspec(512, col["ckv"]), zspec(128, col["kr"]),
                pl.BlockSpec((tm, 128), lambda i: (i % nseq, 0)),
                pl.BlockSpec((tm, 128), lambda i: (i % nseq, 0))]
    names = ["gql", "gkvl"]
    names2 = ["gqn", "gqr", "gkn", "gkr", "gdq", "gdk", "glq", "glk"]
    in_specs += [_layer_vec(vecs[n], l) for n in names] + [wspec(wuq), wspec(wuk), wspec(wuvt)]
    in_specs += [_layer_vec(vecs[n], l) for n in names2]

    def ospec(width):
        return pl.BlockSpec((tm, width), lambda i: (i, 0))

    tspec = pl.BlockSpec((HW, tm), lambda i: (0, i))
    out_shape = [jax.ShapeDtypeStruct((T, 2 * HW), BF16),
                 jax.ShapeDtypeStruct((T, 2 * HW), BF16),
                 jax.ShapeDtypeStruct((HW, T), BF16),
                 jax.ShapeDtypeStruct((T, HW), BF16),
                 jax.ShapeDtypeStruct((T, HW), BF16),
                 jax.ShapeDtypeStruct((DIFF_HEADS * VT_ROWS, T), BF16),
                 jax.ShapeDtypeStruct((T, HW), F32),
                 jax.ShapeDtypeStruct((T, HW), F32),
                 jax.ShapeDtypeStruct((T, HW), F32)]
    vdspec = pl.BlockSpec((DIFF_HEADS * VT_ROWS, tm), lambda i: (0, i))
    out_specs = [ospec(2 * HW), ospec(2 * HW), tspec, ospec(HW), ospec(HW), vdspec, ospec(HW), ospec(HW), ospec(HW)]
    est = 2 * tm * (6 * HW + 1024 + 128 + 256) * 4 + 2 * 2 * 512 * 1536 * 2 + 2 * tm * (12 * HW) * 4
    return pl.pallas_call(
        _prep_body,
        grid=(T // tm,),
        in_specs=in_specs,
        out_specs=out_specs,
        out_shape=out_shape,
        compiler_params=pltpu.CompilerParams(
            dimension_semantics=("arbitrary",), vmem_limit_bytes=_vmem_limit(est)),
        name="prep",
    )(z, z, z, z, z, z, z, z, z, cos_t, sin_t, vecs["gql"], vecs["gkvl"], wuq, wuk, wuvt,
      *[vecs[n] for n in names2])


def _online_update_t(m_ref, l_ref, acc_ref, t, shift, vt):
    m_prev = m_ref[...]
    m_new = jnp.maximum(m_prev, jnp.max(t, axis=0, keepdims=True) + shift)
    alpha = jnp.exp2(m_prev - m_new)
    p = jnp.exp2(t + (shift - m_new))
    if l_ref is not None:
        l_ref[...] = alpha * l_ref[...] + jnp.sum(p, axis=0, keepdims=True)
    acc_ref[...] = alpha * acc_ref[...] + _dot(vt, p.astype(BF16))
    m_ref[...] = m_new


def _init_stats(m_ref, l_ref, acc_ref):
    m_ref[...] = jnp.full(m_ref.shape, NEG_INF, F32)
    if l_ref is not None:
        l_ref[...] = jnp.zeros(l_ref.shape, F32)
    acc_ref[...] = jnp.zeros(acc_ref.shape, F32)


def _causal_sweep(qi, scores, consume, t0_ref, t1_ref):
    t0_ref[...] = scores(0)

    def pair(i, carry):
        t1_ref[...] = scores(2 * i + 1)
        consume(t0_ref, 2 * i, False)
        t0_ref[...] = scores(2 * i + 2)
        consume(t1_ref, 2 * i + 1, False)
        return carry

    lax.fori_loop(0, qi, pair, 0)
    t1_ref[...] = scores(2 * qi + 1)
    consume(t0_ref, 2 * qi, True)
    consume(t1_ref, 2 * qi + 1, True)


def _mla_attn_body(q_ref, k_ref, vt_ref, o_ref, m_ref, l_ref, acc_ref, t0_ref, t1_ref, *, blk):
    qi = pl.program_id(2)
    _init_stats(m_ref, l_ref, acc_ref)

    def keys(kj):
        return pl.ds(pl.multiple_of(kj * blk, blk), blk)

    def scores(kj):
        return _dot_nt(k_ref[0, keys(kj), :], q_ref[0])

    def consume(t_ref, kj, masked):
        t = t_ref[...]
        if masked:
            key = lax.broadcasted_iota(jnp.int32, t.shape, 0) + (kj - 2 * qi) * blk
            qry = lax.broadcasted_iota(jnp.int32, t.shape, 1)
            t = jnp.where(key <= qry, t, NEG_INF)
        _online_update_t(m_ref, l_ref, acc_ref, t, 0.0, vt_ref[:, keys(kj)])

    _causal_sweep(qi, scores, consume, t0_ref, t1_ref)
    o_ref[0] = (acc_ref[...] * (1.0 / l_ref[...])).T.astype(o_ref.dtype)


def _mla_attn(qm, km, vmt, *, blk):
    B, S, _ = qm.shape
    bq = 2 * blk
    est = 2 * (bq * 256 * 2 + S * 256 * 2 + S * 128 * 2 + bq * 128 * 2) + 3 * bq * 128 * 4 + 6 * blk * bq * 4
    return pl.pallas_call(
        functools.partial(_mla_attn_body, blk=blk),
        grid=(B, MLA_HEADS, S // bq),
        in_specs=[pl.BlockSpec((1, bq, 256), lambda b, h, i: (b, i, h)),
                  pl.BlockSpec((1, S, 256), lambda b, h, i: (b, 0, h)),
                  pl.BlockSpec((128, S), lambda b, h, i: (h, b))],
        out_specs=pl.BlockSpec((1, bq, 128), lambda b, h, i: (b, i, h)),
        out_shape=jax.ShapeDtypeStruct((B, S, MLA_HEADS * MLA_V), BF16),
        scratch_shapes=[pltpu.VMEM((1, bq), F32), pltpu.VMEM((1, bq), F32), pltpu.VMEM((128, bq), F32),
                        pltpu.VMEM((blk, bq), F32), pltpu.VMEM((blk, bq), F32)],
        compiler_params=pltpu.CompilerParams(
            dimension_semantics=("arbitrary", "arbitrary", "arbitrary"), vmem_limit_bytes=_vmem_limit(est)),
        name="mla_attn",
    )(qm, km, vmt)


def _diff_attn_body(sc_ref, q_ref, k_ref, vt_ref, lq1_ref, lk1_ref, lq2_ref, lk2_ref, g_ref, o_ref,
                    m_ref, acc_ref, qq_ref, t0_ref, t1_ref, *, blk):
    l_ref = None
    bq = 2 * blk
    h = pl.program_id(1)
    qi = pl.program_id(2)
    slope = sc_ref[h]
    lam_init = sc_ref[DIFF_HEADS]
    q = q_ref[0]
    lo = lax.broadcasted_iota(jnp.int32, q.shape, 1) < DIFF_D
    zero = jnp.zeros(q.shape, q.dtype)
    qq_ref[0:bq, :] = jnp.where(lo, q, zero)
    qq_ref[bq:2 * bq, :] = jnp.where(lo, zero, q)
    _init_stats(m_ref, l_ref, acc_ref)
    key = lax.broadcasted_iota(jnp.int32, (blk, 2 * bq), 0)
    col = lax.broadcasted_iota(jnp.int32, (blk, 2 * bq), 1)
    qry = jnp.where(col >= bq, col - bq, col)

    def keys(kj):
        return pl.ds(pl.multiple_of(kj * blk, blk), blk)

    def scores(kj):
        return _dot_nt(k_ref[0, keys(kj), :], qq_ref[...]) + slope * key.astype(F32)

    def consume(t_ref, kj, masked):
        t = t_ref[...]
        if masked:
            t = jnp.where(key + (kj - 2 * qi) * blk <= qry, t, NEG_INF)
        shift = slope * ((kj - 2 * qi) * blk).astype(F32)
        _online_update_t(m_ref, l_ref, acc_ref, t, shift, vt_ref[:, keys(kj)])

    _causal_sweep(qi, scores, consume, t0_ref, t1_ref)

    lam = (jnp.exp(jnp.sum(lq1_ref[...] * lk1_ref[...], axis=-1, keepdims=True))
           - jnp.exp(jnp.sum(lq2_ref[...] * lk2_ref[...], axis=-1, keepdims=True)) + lam_init)
    acc = acc_ref[...]
    on = acc[:DIFF_V] * (1.0 / acc[DIFF_V:DIFF_V + 1])
    o = on[:, :bq] - lam * on[:, bq:]
    o = o * lax.rsqrt(jnp.sum(o * o, axis=0, keepdims=True) * (1.0 / DIFF_V) + EPS)
    o_ref[0] = ((o * g_ref[...]) * (1.0 - lam_init)).T.astype(o_ref.dtype)


def _diff_attn(sc, qd, kd, vdt, lq1, lk1, lq2, lk2, gout, l, *, blk):
    B, S, _ = qd.shape
    bq = 2 * blk
    est = 2 * (2 * bq * 128 * 2 + 2 * S * 128 * 2) + 4 * bq * 128 * 4 + 6 * blk * 2 * bq * 4
    return pl.pallas_call(
        functools.partial(_diff_attn_body, blk=blk),
        grid=(B, DIFF_HEADS, S // bq),
        in_specs=[pl.BlockSpec(memory_space=pltpu.SMEM),
                  pl.BlockSpec((1, bq, 128), lambda b, h, i: (b, i, h)),
                  pl.BlockSpec((1, S, 128), lambda b, h, i: (b, 0, h)),
                  pl.BlockSpec((VT_ROWS, S), lambda b, h, i: (h, b)),
                  _layer_vec(lq1, l), _layer_vec(lk1, l), _layer_vec(lq2, l), _layer_vec(lk2, l),
                  _layer_vec(gout, l)],
        out_specs=pl.BlockSpec((1, bq, 128), lambda b, h, i: (b, i, h)),
        out_shape=jax.ShapeDtypeStruct((B, S, DIFF_HEADS * DIFF_V), BF16),
        scratch_shapes=[pltpu.VMEM((1, 2 * bq), F32),
                        pltpu.VMEM((VT_ROWS, 2 * bq), F32), pltpu.VMEM((2 * bq, 128), BF16),
                        pltpu.VMEM((blk, 2 * bq), F32), pltpu.VMEM((blk, 2 * bq), F32)],
        compiler_params=pltpu.CompilerParams(
            dimension_semantics=("arbitrary", "arbitrary", "arbitrary"), vmem_limit_bytes=_vmem_limit(est)),
        name="diff_attn",
    )(sc, qd, kd, vdt, lq1, lk1, lq2, lk2, gout)


def _dil_attn_body(sc_ref, q_ref, k_ref, v_ref, o_ref, qs_ref, ks_ref, vs_ref, op_ref, lse_ref, *, S):
    h = pl.program_id(1)
    slope = sc_ref[h]
    blk = DIL_BLOCK
    ii = lax.broadcasted_iota(jnp.int32, (blk, 2 * blk), 0)
    jj = lax.broadcasted_iota(jnp.int32, (blk, 2 * blk), 1)
    back = ii + blk - jj
    in_band = (back >= 0) & (back <= blk)
    ones = jnp.ones((blk, V7X_LANES), BF16)

    for p, (window, r) in enumerate(DIL_PATTERNS):
        assert window // r == blk
        L = S // r
        nb = L // blk
        Lp = L + blk
        bias = jnp.where(in_band, (-slope * float(r)) * back.astype(F32), NEG_INF)

        def src_rows(start, r=r):
            return pl.ds(start, blk, stride=r) if r > 1 else pl.ds(start, blk)

        def zero_pad(c, carry, Lp=Lp):
            pad = pl.ds(pl.multiple_of(c * Lp, blk), blk)
            ks_ref[pad, :] = jnp.zeros((blk, V7X_LANES), BF16)
            vs_ref[pad, :] = jnp.zeros((blk, 2 * V7X_LANES), BF16)
            return carry

        lax.fori_loop(0, r, zero_pad, 0)

        def gather(idx, carry, r=r, nb=nb, L=L, Lp=Lp, src_rows=src_rows):
            c = idx // nb
            n = idx % nb
            src = src_rows(c + n * (blk * r))
            dst = pl.ds(pl.multiple_of(c * Lp + (n + 1) * blk, blk), blk)
            qs_ref[pl.ds(pl.multiple_of(c * L + n * blk, blk), blk), :] = q_ref[0, src, :].astype(BF16)
            ks_ref[dst, :] = k_ref[0, src, :].astype(BF16)
            vs_ref[dst, :] = jnp.concatenate([v_ref[0, src, :].astype(BF16), ones], axis=1)
            return carry

        lax.fori_loop(0, r * nb, gather, 0)

        def blocks(it, carry, r=r, nb=nb, L=L, Lp=Lp, p=p, bias=bias, src_rows=src_rows):
            for u in range(DIL_UNROLL):
                idx = it * DIL_UNROLL + u
                c = idx // nb
                n = idx % nb
                q = qs_ref[pl.ds(pl.multiple_of(c * L + n * blk, blk), blk), :]
                keys = pl.ds(pl.multiple_of(c * Lp + n * blk, blk), 2 * blk)
                s = _dot_nt(q, ks_ref[keys, :]) + bias
                s = jnp.where((n >= 1) | (jj >= blk), s, NEG_INF)
                m = jnp.max(s, axis=-1, keepdims=True)
                e = jnp.exp2(s - m)
                ov = _dot(e.astype(BF16), vs_ref[keys, :])
                den = ov[:, V7X_LANES:]
                dst = src_rows(c + n * (blk * r))
                op_ref[p, dst, :] = ov[:, :V7X_LANES] * (1.0 / den)
                lse_ref[p, dst, :] = m + jnp.log2(den)
            return carry

        lax.fori_loop(0, (r * nb) // DIL_UNROLL, blocks, 0)

    chunk = 256

    def combine(ci, carry):
        rows = pl.ds(pl.multiple_of(ci * chunk, chunk), chunk)
        l0 = lse_ref[0, rows, :]
        l1 = lse_ref[1, rows, :]
        l2 = lse_ref[2, rows, :]
        mm = jnp.maximum(jnp.maximum(l0, l1), l2)
        w0 = jnp.exp2(l0 - mm)
        w1 = jnp.exp2(l1 - mm)
        w2 = jnp.exp2(l2 - mm)
        num = w0 * op_ref[0, rows, :] + w1 * op_ref[1, rows, :] + w2 * op_ref[2, rows, :]
        o_ref[0, rows, :] = (num * (1.0 / (w0 + w1 + w2))).astype(o_ref.dtype)
        return carry

    lax.fori_loop(0, S // chunk, combine, 0)


def _dil_attn(sc, ql, kl, vl):
    B, S, _ = ql.shape
    r_max = max(r for _, r in DIL_PATTERNS)
    assert S % (DIL_BLOCK * r_max) == 0 and (S // DIL_BLOCK) % DIL_UNROLL == 0
    assert len(DIL_PATTERNS) == 3
    pad_rows = S + DIL_BLOCK * r_max
    est = (2 * (3 * S * 128 * 4 + S * 128 * 2) + 6 * S * 128 * 4
           + S * 128 * 2 + pad_rows * 128 * 2 + pad_rows * 256 * 2)
    return pl.pallas_call(
        functools.partial(_dil_attn_body, S=S),
        grid=(B, DIL_HEADS),
        in_specs=[pl.BlockSpec(memory_space=pltpu.SMEM),
                  pl.BlockSpec((1, S, 128), lambda b, h: (b, 0, h)),
                  pl.BlockSpec((1, S, 128), lambda b, h: (b, 0, h)),
                  pl.BlockSpec((1, S, 128), lambda b, h: (b, 0, h))],
        out_specs=pl.BlockSpec((1, S, 128), lambda b, h: (b, 0, h)),
        out_shape=jax.ShapeDtypeStruct((B, S, DIL_HEADS * DIL_D), BF16),
        scratch_shapes=[pltpu.VMEM((S, 128), BF16), pltpu.VMEM((pad_rows, 128), BF16),
                        pltpu.VMEM((pad_rows, 256), BF16),
                        pltpu.VMEM((3, S, 128), F32), pltpu.VMEM((3, S, 128), F32)],
        compiler_params=pltpu.CompilerParams(
            dimension_semantics=("arbitrary", "arbitrary"), vmem_limit_bytes=_vmem_limit(est)),
        name="dil_attn",
    )(sc, ql, kl, vl)


def _merge_body(x_ref, gz_ref, gb_ref, oa_ref, ob_ref, oc_ref, wa_ref, wb_ref, wc_ref, wo_ref, o_ref):
    D = x_ref.shape[-1]

    def gate(b):
        return jax.nn.sigmoid(gz_ref[:, b * D:(b + 1) * D].astype(F32) + gb_ref[:, b * D:(b + 1) * D])

    merged = (gate(0) * _dot(oa_ref[...], wa_ref[...])
              + gate(1) * _dot(ob_ref[...], wb_ref[...])
              + gate(2) * _dot(oc_ref[...], wc_ref[...]))
    o_ref[...] = x_ref[...] + _dot(merged.astype(BF16), wo_ref[...])


def _merge(x, z, gate_bias, oa, ob, oc, wa, wb, wc, wo, l, *, tm):
    T, D = x.shape

    def resident(a):
        return pl.BlockSpec((None,) + a.shape[1:], lambda i: (l, 0, 0), pipeline_mode=pl.Buffered(1))

    def rows(width):
        return pl.BlockSpec((tm, width), lambda i: (i, 0))

    zsz = jnp.dtype(z.dtype).itemsize
    est = (2 * tm * D * 4 * 2 + 2 * tm * N_BRANCH * D * zsz + 2 * 3 * tm * HW * 2
           + (3 * HW * D + D * D) * 2 + N_BRANCH * D * 4 + 4 * tm * D * 4)
    return pl.pallas_call(
        _merge_body,
        grid=(T // tm,),
        in_specs=[rows(D), rows(N_BRANCH * D), resident(gate_bias), rows(HW), rows(HW), rows(HW),
                  resident(wa), resident(wb), resident(wc), resident(wo)],
        out_specs=rows(D),
        out_shape=jax.ShapeDtypeStruct((T, D), F32),
        compiler_params=pltpu.CompilerParams(
            dimension_semantics=("arbitrary",), vmem_limit_bytes=_vmem_limit(est)),
        name="merge",
    )(x, z, gate_bias, oa, ob, oc, wa, wb, wc, wo)


def _ffn_up_body(x_ref, g_ref, wg_ref, wu_ref, cwg_ref, cwu_ref, cbg_ref, cbu_ref, o_ref, h_ref, *, tm, tiles_per_seq):
    i = pl.program_id(0)

    @pl.when(pl.program_id(1) == 0)
    def _():
        @pl.when(i % tiles_per_seq == 0)
        def _():
            h_ref[0:HALO, :] = jnp.zeros((HALO, h_ref.shape[1]), BF16)

        @pl.when(i % tiles_per_seq != 0)
        def _():
            h_ref[0:HALO, :] = h_ref[tm:tm + HALO, :]

        x = x_ref[...]
        h_ref[HALO:HALO + tm, :] = (_rms(x, x.shape[-1]) * g_ref[...]).astype(BF16)

    h = h_ref[...]

    def conv(u, cw_ref, cb_ref):
        return (u[HALO:HALO + tm] * cw_ref[2:3, :] + u[HALO - 1:HALO - 1 + tm] * cw_ref[1:2, :]
                + u[HALO - 2:HALO - 2 + tm] * cw_ref[0:1, :] + cb_ref[...])

    gate = conv(_dot(h, wg_ref[...]), cwg_ref, cbg_ref)
    up = conv(_dot(h, wu_ref[...]), cwu_ref, cbu_ref)
    o_ref[...] = (gate * jax.nn.sigmoid(gate) * up).astype(o_ref.dtype)


def _ffn_up(x, g, w_up, conv_w, conv_b, l, *, S, tm, tn):
    T, K = x.shape
    F = w_up.shape[2] // 2
    nj = F // tn
    est = 2 * tm * K * 4 + (tm + HALO) * K * 2 + 2 * 2 * K * tn * 2 + 2 * tm * tn * 2 + 6 * (tm + HALO) * tn * 4
    return pl.pallas_call(
        functools.partial(_ffn_up_body, tm=tm, tiles_per_seq=S // tm),
        grid=(T // tm, nj),
        in_specs=[pl.BlockSpec((tm, K), lambda i, j: (i, 0)),
                  _layer_vec(g, l),
                  pl.BlockSpec((None, K, tn), lambda i, j: (l, 0, j)),
                  pl.BlockSpec((None, K, tn), lambda i, j: (l, 0, j + nj)),
                  pl.BlockSpec((None, CONV_W, tn), lambda i, j: (l, 0, j)),
                  pl.BlockSpec((None, CONV_W, tn), lambda i, j: (l, 0, j + nj)),
                  pl.BlockSpec((None, 1, tn), lambda i, j: (l, 0, j)),
                  pl.BlockSpec((None, 1, tn), lambda i, j: (l, 0, j + nj))],
        out_specs=pl.BlockSpec((tm, tn), lambda i, j: (i, j)),
        out_shape=jax.ShapeDtypeStruct((T, F), BF16),
        scratch_shapes=[pltpu.VMEM((tm + HALO, K), BF16)],
        compiler_params=pltpu.CompilerParams(
            dimension_semantics=("arbitrary", "arbitrary"), vmem_limit_bytes=_vmem_limit(est)),
        name="ffn_up",
    )(x, g, w_up, w_up, conv_w, conv_w, conv_b, conv_b)


def _ffn_down_body(x_ref, a_ref, w_ref, o_ref):
    o_ref[...] = x_ref[...] + _dot(a_ref[...], w_ref[...])


def _ffn_down(x, act, w, l, *, tm):
    T, D = x.shape
    F = act.shape[1]
    est = 2 * (2 * tm * D * 4 + tm * F * 2) + F * D * 2
    return pl.pallas_call(
        _ffn_down_body,
        grid=(T // tm,),
        in_specs=[pl.BlockSpec((tm, D), lambda i: (i, 0)),
                  pl.BlockSpec((tm, F), lambda i: (i, 0)),
                  pl.BlockSpec((None, F, D), lambda i: (l, 0, 0), pipeline_mode=pl.Buffered(1))],
        out_specs=pl.BlockSpec((tm, D), lambda i: (i, 0)),
        out_shape=jax.ShapeDtypeStruct((T, D), F32),
        compiler_params=pltpu.CompilerParams(
            dimension_semantics=("arbitrary",), vmem_limit_bytes=_vmem_limit(est)),
        name="ffn_down",
    )(x, act, w)


def _rope_lanes(t):
    half = MLA_ROPE // 2
    z = jnp.zeros(t.shape[:-1] + (half,), t.dtype)
    return jnp.concatenate([t[..., :half], z, t[..., half:], z], axis=-1)


def _z_columns(D):
    col = {"gz": 0}
    c = N_BRANCH * D
    for name in ("dq", "dk", "dv", "lq", "lk", "lv"):
        col[name] = c
        c += HW
    col["cq"] = c
    col["ckv"] = c + 512
    col["kr"] = c + 1024
    return col, c + 1024 + 128


def _w_in_layout(w, D, n_pad):
    o_cq, o_ckv, o_kr = 0, 512, 1024
    o_d = 1024 + MLA_ROPE
    o_gz = o_d + 6 * HW
    parts = [w[..., o_gz:o_gz + N_BRANCH * D], w[..., o_d:o_gz], w[..., o_cq:o_ckv], w[..., o_ckv:o_kr],
             _rope_lanes(w[..., o_kr:o_d])]
    used = N_BRANCH * D + 6 * HW + 1024 + 128
    parts.append(jnp.zeros(w.shape[:-1] + (n_pad - used,), w.dtype))
    return jnp.concatenate([p.astype(BF16) for p in parts], axis=-1)


def _w_uq_layout(w):
    r = w.reshape(w.shape[:-1] + (MLA_HEADS, MLA_NOPE + MLA_ROPE))
    out = jnp.concatenate([r[..., :MLA_NOPE], _rope_lanes(r[..., MLA_NOPE:])], axis=-1)
    return out.reshape(w.shape[:-1] + (MLA_HEADS * 2 * V7X_LANES,)).astype(BF16)


def _rope_tables(S):
    half = MLA_ROPE // 2
    inv = ROPE_THETA ** (-jnp.arange(half, dtype=F32) / half)
    ang = jnp.arange(S, dtype=F32)[:, None] * inv[None, :]
    cos, sin = jnp.cos(ang), jnp.sin(ang)
    z = jnp.zeros_like(cos)
    return jnp.concatenate([cos, z, cos, z], axis=-1), jnp.concatenate([-sin, z, sin, z], axis=-1)


def kernel(x, attn_norm_g, w_in, mla_q_lora_norm_g, mla_kv_lora_norm_g, mla_w_uq, mla_w_uk, mla_w_uv, mla_q_gain, mla_k_gain, diff_q_gain, diff_k_gain, diff_lam_q1, diff_lam_k1, diff_lam_q2, diff_lam_k2, diff_out_norm_g, dil_q_gain, dil_k_gain, gate_bias, w_br_mla, w_br_diff, w_br_dil, w_out, ffn_norm_g, ffn_w_up, ffn_conv_w, ffn_conv_b, ffn_w_down):
    B, S, D = x.shape
    depth = w_in.shape[0]
    T = B * S
    col, n_used = _z_columns(D)
    n_z = -(-n_used // 1024) * 1024
    cos_t, sin_t = _rope_tables(S)
    slopes = LOG2E * 2.0 ** (-8.0 * jnp.arange(1, DIFF_HEADS + DIL_HEADS + 1, dtype=F32) / (DIFF_HEADS + DIL_HEADS))
    s_mla = LOG2E * (MLA_NOPE + MLA_ROPE) ** -0.5
    rows = lambda v: v.reshape(depth, 1, -1).astype(F32)

    w_in_p = _w_in_layout(w_in, D, n_z)
    w_uq_p = _w_uq_layout(mla_w_uq)
    w_uk_p = mla_w_uk.astype(BF16)
    w_uvt_p = jnp.swapaxes(mla_w_uv, 1, 2).astype(BF16)
    w_bra, w_brb, w_brc, w_out_p = (w.astype(BF16) for w in (w_br_mla, w_br_diff, w_br_dil, w_out))
    w_up_p, w_down_p = ffn_w_up.astype(BF16), ffn_w_down.astype(BF16)
    vecs = {
        "gql": rows(mla_q_lora_norm_g), "gkvl": rows(mla_kv_lora_norm_g),
        "gqn": rows(mla_q_gain[:, :MLA_NOPE] * s_mla), "gqr": rows(_rope_lanes(mla_q_gain[:, MLA_NOPE:]) * s_mla),
        "gkn": rows(mla_k_gain[:, :MLA_NOPE]), "gkr": rows(_rope_lanes(mla_k_gain[:, MLA_NOPE:])),
        "gdq": rows(jnp.tile(diff_q_gain, (1, 2)) * (LOG2E * DIFF_D ** -0.5)), "gdk": rows(jnp.tile(diff_k_gain, (1, 2))),
        "glq": rows(dil_q_gain * (LOG2E * DIL_D ** -0.5)), "glk": rows(dil_k_gain),
    }
    g_attn, g_ffn, g_bias, c_b = rows(attn_norm_g), rows(ffn_norm_g), rows(gate_bias), rows(ffn_conv_b)
    lam_q1, lam_k1, lam_q2, lam_k2 = rows(diff_lam_q1), rows(diff_lam_k1), rows(diff_lam_q2), rows(diff_lam_k2)
    g_dout = diff_out_norm_g.reshape(depth, DIFF_V, 1).astype(F32)
    sc_dil = jnp.concatenate([slopes[1::2], jnp.zeros((2,), F32)])

    xf = x.reshape(T, D)
    b3 = lambda a: a.reshape(B, S, a.shape[-1])
    flat = lambda a: a.reshape(T, a.shape[-1])
    for l in range(depth):
        lam_init = 0.8 - 0.6 * math.exp(-0.3 * l)
        z = _norm_matmul(xf, g_attn, w_in_p, l, tm=1024, tn=1024, out_dtype=BF16)
        qm, km, vmt, qd, kd, vdt, ql, kl, vl = _prep(
            z, cos_t, sin_t, vecs, w_uq_p, w_uk_p, w_uvt_p, l, S=S, tm=256, col=col)
        o_mla = _mla_attn(b3(qm), b3(km), vmt, blk=512)
        sc_diff = jnp.concatenate([slopes[0::2], jnp.full((2,), lam_init, F32)])
        o_diff = _diff_attn(sc_diff, b3(qd), b3(kd), vdt, lam_q1, lam_k1, lam_q2, lam_k2, g_dout, l, blk=512)
        o_dil = _dil_attn(sc_dil, b3(ql), b3(kl), b3(vl))
        xf = _merge(xf, z, g_bias, flat(o_mla), flat(o_diff), flat(o_dil), w_bra, w_brb, w_brc, w_out_p, l, tm=256)
        act = _ffn_up(xf, g_ffn, w_up_p, ffn_conv_w, c_b, l, S=S, tm=1024, tn=512)
        xf = _ffn_down(xf, act, w_down_p, l, tm=256)
    return xf.reshape(B, S, D)
```

```python
import functools
import math

import jax
import jax.numpy as jnp
from jax import lax
from jax.experimental import pallas as pl
from jax.experimental.pallas import tpu as pltpu

F32 = jnp.float32
BF16 = jnp.bfloat16
EPS = 1e-6
NEG_INF = float("-inf")
LOG2E = math.log2(math.e)

V7X_LANES = 128
V7X_VMEM_BYTES = 64 * 1024 * 1024
BF16_ROWS_PER_VREG = 16

MLA_HEADS = 6
MLA_NOPE = 128
MLA_ROPE = 64
MLA_V = 128
ROPE_THETA = 10000.0
DIFF_HEADS = 6
DIFF_D = 64
DIFF_V = 128
DIL_HEADS = 6
DIL_D = 128
DIL_PATTERNS = ((128, 1), (512, 4), (2048, 16))
DIL_BLOCK = 128
DIL_UNROLL = 16
N_BRANCH = 3
CONV_W = 3
HALO = BF16_ROWS_PER_VREG
HW = 6 * 128
VT_ROWS = DIFF_V + BF16_ROWS_PER_VREG


def _vmem_limit(block_bytes):
    return int(min(V7X_VMEM_BYTES - (4 << 20), block_bytes + (16 << 20)))


def _dot(a, b):
    return jnp.dot(a, b, preferred_element_type=F32)


def _dot_nt(a, b):
    return lax.dot_general(a, b, (((1,), (1,)), ((), ())), preferred_element_type=F32)


def _rms(x, n):
    return x * lax.rsqrt(jnp.sum(x * x, axis=-1, keepdims=True) * (1.0 / n) + EPS)


def _layer_vec(a, l):
    return pl.BlockSpec((None,) + a.shape[1:], lambda *_: (l, 0, 0))


def _norm_matmul_body(x_ref, g_ref, w_ref, o_ref, h_ref):
    @pl.when(pl.program_id(1) == 0)
    def _():
        x = x_ref[...]
        h_ref[...] = (_rms(x, x.shape[-1]) * g_ref[...]).astype(BF16)

    o_ref[...] = _dot(h_ref[...], w_ref[...]).astype(o_ref.dtype)


def _norm_matmul(x, g, w, l, *, tm, tn, out_dtype):
    T, K = x.shape
    N = w.shape[2]
    osz = jnp.dtype(out_dtype).itemsize
    est = 2 * tm * K * 4 + tm * K * 2 + 2 * K * tn * 2 + 2 * tm * tn * osz
    return pl.pallas_call(
        _norm_matmul_body,
        grid=(T // tm, N // tn),
        in_specs=[pl.BlockSpec((tm, K), lambda i, j: (i, 0)),
                  _layer_vec(g, l),
                  pl.BlockSpec((None, K, tn), lambda i, j: (l, 0, j))],
        out_specs=pl.BlockSpec((tm, tn), lambda i, j: (i, j)),
        out_shape=jax.ShapeDtypeStruct((T, N), out_dtype),
        scratch_shapes=[pltpu.VMEM((tm, K), BF16)],
        compiler_params=pltpu.CompilerParams(
            dimension_semantics=("arbitrary", "arbitrary"), vmem_limit_bytes=_vmem_limit(est)),
        name="norm_matmul",
    )(x, g, w)


def _rope(xb, c, s):
    return xb * c + pltpu.roll(xb, 64, 1) * s


def _prep_body(dq_ref, dk_ref, dv_ref, lq_ref, lk_ref, lv_ref, cq_ref, ckv_ref, kr_ref,
               cos_ref, sin_ref, gql_ref, gkvl_ref, wuq_ref, wuk_ref, wuvt_ref,
               gqn_ref, gqr_ref, gkn_ref, gkr_ref, gdq_ref, gdk_ref, glq_ref, glk_ref,
               qm_ref, km_ref, vmt_ref, qd_ref, kd_ref, vdt_ref, ql_ref, kl_ref, vl_ref):
    c = cos_ref[...]
    s = sin_ref[...]
    tm = c.shape[0]
    f32 = lambda ref: ref[...].astype(F32)

    cqn = (_rms(f32(cq_ref), cq_ref.shape[-1]) * gql_ref[...]).astype(BF16)
    q = _dot(cqn, wuq_ref[...])
    for h in range(MLA_HEADS):
        a = 2 * V7X_LANES * h
        qm_ref[:, a:a + 128] = (_rms(q[:, a:a + 128], MLA_NOPE) * gqn_ref[...]).astype(BF16)
        qr = _rms(q[:, a + 128:a + 256], MLA_ROPE) * gqr_ref[...]
        qm_ref[:, a + 128:a + 256] = _rope(qr, c, s).astype(BF16)
    ckvn = (_rms(f32(ckv_ref), ckv_ref.shape[-1]) * gkvl_ref[...]).astype(BF16)
    kn = _dot(ckvn, wuk_ref[...])
    kr = _rope(_rms(f32(kr_ref), MLA_ROPE) * gkr_ref[...], c, s).astype(BF16)
    for h in range(MLA_HEADS):
        a = V7X_LANES * h
        km_ref[:, 2 * a:2 * a + 128] = (_rms(kn[:, a:a + 128], MLA_NOPE) * gkn_ref[...]).astype(BF16)
        km_ref[:, 2 * a + 128:2 * a + 256] = kr
    vmt_ref[...] = _dot_nt(wuvt_ref[...], ckvn).astype(BF16)

    lo = lax.broadcasted_iota(jnp.int32, (tm, V7X_LANES), 1) < DIFF_D

    def half_rms(xh):
        sq = xh * xh
        s_lo = jnp.sum(jnp.where(lo, sq, 0.0), axis=-1, keepdims=True)
        s_hi = jnp.sum(jnp.where(lo, 0.0, sq), axis=-1, keepdims=True)
        inv = jnp.where(lo, lax.rsqrt(s_lo * (1.0 / DIFF_D) + EPS), lax.rsqrt(s_hi * (1.0 / DIFF_D) + EPS))
        return xh * inv

    for h in range(DIFF_HEADS):
        a = V7X_LANES * h
        qd_ref[:, a:a + 128] = (half_rms(dq_ref[:, a:a + 128].astype(F32)) * gdq_ref[...]).astype(BF16)
        kd_ref[:, a:a + 128] = (half_rms(dk_ref[:, a:a + 128].astype(F32)) * gdk_ref[...]).astype(BF16)
    vdt = f32(dv_ref).T.astype(BF16)
    for h in range(DIFF_HEADS):
        vdt_ref[VT_ROWS * h:VT_ROWS * h + DIFF_V, :] = vdt[DIFF_V * h:DIFF_V * (h + 1)]
        vdt_ref[VT_ROWS * h + DIFF_V:VT_ROWS * (h + 1), :] = jnp.ones((VT_ROWS - DIFF_V, tm), BF16)

    for h in range(DIL_HEADS):
        a = V7X_LANES * h
        ql_ref[:, a:a + 128] = _rms(lq_ref[:, a:a + 128].astype(F32), DIL_D) * glq_ref[...]
        kl_ref[:, a:a + 128] = _rms(lk_ref[:, a:a + 128].astype(F32), DIL_D) * glk_ref[...]
    vl_ref[...] = f32(lv_ref)


def _prep(z, cos_t, sin_t, vecs, wuq, wuk, wuvt, l, *, S, tm, col):
    T = z.shape[0]
    nseq = S // tm

    def zspec(width, start):
        assert start % width == 0
        return pl.BlockSpec((tm, width), lambda i, blk=start // width: (i, blk))

    def wspec(a):
        return pl.BlockSpec((None,) + a.shape[1:], lambda i: (l, 0, 0))

    in_specs = [zspec(HW, col["dq"]), zspec(HW, col["dk"]), zspec(HW, col["dv"]),
                zspec(HW, col["lq"]), zspec(HW, col["lk"]), zspec(HW, col["lv"]),
                zspec(512, col["cq"]), zspec(512, col["ckv"]), zspec(128, col["kr"]),
                pl.BlockSpec((tm, 128), lambda i: (i % nseq, 0)),
                pl.BlockSpec((tm, 128), lambda i: (i % nseq, 0))]
    names = ["gql", "gkvl"]
    names2 = ["gqn", "gqr", "gkn", "gkr", "gdq", "gdk", "glq", "glk"]
    in_specs += [_layer_vec(vecs[n], l) for n in names] + [wspec(wuq), wspec(wuk), wspec(wuvt)]
    in_specs += [_layer_vec(vecs[n], l) for n in names2]

    def ospec(width):
        return pl.BlockSpec((tm, width), lambda i: (i, 0))

    tspec = pl.BlockSpec((HW, tm), lambda i: (0, i))
    out_shape = [jax.ShapeDtypeStruct((T, 2 * HW), BF16),
                 jax.ShapeDtypeStruct((T, 2 * HW), BF16),
                 jax.ShapeDtypeStruct((HW, T), BF16),
                 jax.ShapeDtypeStruct((T, HW), BF16),
                 jax.ShapeDtypeStruct((T, HW), BF16),
                 jax.ShapeDtypeStruct((DIFF_HEADS * VT_ROWS, T), BF16),
                 jax.ShapeDtypeStruct((T, HW), F32),
                 jax.ShapeDtypeStruct((T, HW), F32),
                 jax.ShapeDtypeStruct((T, HW), F32)]
    vdspec = pl.BlockSpec((DIFF_HEADS * VT_ROWS, tm), lambda i: (0, i))
    out_specs = [ospec(2 * HW), ospec(2 * HW), tspec, ospec(HW), ospec(HW), vdspec, ospec(HW), ospec(HW), ospec(HW)]
    est = 2 * tm * (6 * HW + 1024 + 128 + 256) * 4 + 2 * 2 * 512 * 1536 * 2 + 2 * tm * (12 * HW) * 4
    return pl.pallas_call(
        _prep_body,
        grid=(T // tm,),
        in_specs=in_specs,
        out_specs=out_specs,
        out_shape=out_shape,
        compiler_params=pltpu.CompilerParams(
            dimension_semantics=("arbitrary",), vmem_limit_bytes=_vmem_limit(est)),
        name="prep",
    )(z, z, z, z, z, z, z, z, z, cos_t, sin_t, vecs["gql"], vecs["gkvl"], wuq, wuk, wuvt,
      *[vecs[n] for n in names2])


def _online_update_t(m_ref, l_ref, acc_ref, t, shift, vt):
    m_prev = m_ref[...]
    m_new = jnp.maximum(m_prev, jnp.max(t, axis=0, keepdims=True) + shift)
    alpha = jnp.exp2(m_prev - m_new)
    p = jnp.exp2(t + (shift - m_new))
    if l_ref is not None:
        l_ref[...] = alpha * l_ref[...] + jnp.sum(p, axis=0, keepdims=True)
    acc_ref[...] = alpha * acc_ref[...] + _dot(vt, p.astype(BF16))
    m_ref[...] = m_new


def _init_stats(m_ref, l_ref, acc_ref):
    m_ref[...] = jnp.full(m_ref.shape, NEG_INF, F32)
    if l_ref is not None:
        l_ref[...] = jnp.zeros(l_ref.shape, F32)
    acc_ref[...] = jnp.zeros(acc_ref.shape, F32)


def _causal_sweep(qi, scores, consume, t0_ref, t1_ref):
    t0_ref[...] = scores(0)

    def pair(i, carry):
        t1_ref[...] = scores(2 * i + 1)
        consume(t0_ref, 2 * i, False)
        t0_ref[...] = scores(2 * i + 2)
        consume(t1_ref, 2 * i + 1, False)
        return carry

    lax.fori_loop(0, qi, pair, 0)
    t1_ref[...] = scores(2 * qi + 1)
    consume(t0_ref, 2 * qi, True)
    consume(t1_ref, 2 * qi + 1, True)


def _mla_attn_body(q_ref, k_ref, vt_ref, o_ref, m_ref, l_ref, acc_ref, t0_ref, t1_ref, *, blk):
    qi = pl.program_id(2)
    _init_stats(m_ref, l_ref, acc_ref)

    def keys(kj):
        return pl.ds(pl.multiple_of(kj * blk, blk), blk)

    def scores(kj):
        return _dot_nt(k_ref[0, keys(kj), :], q_ref[0])

    def consume(t_ref, kj, masked):
        t = t_ref[...]
        if masked:
            key = lax.broadcasted_iota(jnp.int32, t.shape, 0) + (kj - 2 * qi) * blk
            qry = lax.broadcasted_iota(jnp.int32, t.shape, 1)
            t = jnp.where(key <= qry, t, NEG_INF)
        _online_update_t(m_ref, l_ref, acc_ref, t, 0.0, vt_ref[:, keys(kj)])

    _causal_sweep(qi, scores, consume, t0_ref, t1_ref)
    o_ref[0] = (acc_ref[...] * (1.0 / l_ref[...])).T.astype(o_ref.dtype)


def _mla_attn(qm, km, vmt, *, blk):
    B, S, _ = qm.shape
    bq = 2 * blk
    est = 2 * (bq * 256 * 2 + S * 256 * 2 + S * 128 * 2 + bq * 128 * 2) + 3 * bq * 128 * 4 + 6 * blk * bq * 4
    return pl.pallas_call(
        functools.partial(_mla_attn_body, blk=blk),
        grid=(B, MLA_HEADS, S // bq),
        in_specs=[pl.BlockSpec((1, bq, 256), lambda b, h, i: (b, i, h)),
                  pl.BlockSpec((1, S, 256), lambda b, h, i: (b, 0, h)),
                  pl.BlockSpec((128, S), lambda b, h, i: (h, b))],
        out_specs=pl.BlockSpec((1, bq, 128), lambda b, h, i: (b, i, h)),
        out_shape=jax.ShapeDtypeStruct((B, S, MLA_HEADS * MLA_V), BF16),
        scratch_shapes=[pltpu.VMEM((1, bq), F32), pltpu.VMEM((1, bq), F32), pltpu.VMEM((128, bq), F32),
                        pltpu.VMEM((blk, bq), F32), pltpu.VMEM((blk, bq), F32)],
        compiler_params=pltpu.CompilerParams(
            dimension_semantics=("arbitrary", "arbitrary", "arbitrary"), vmem_limit_bytes=_vmem_limit(est)),
        name="mla_attn",
    )(qm, km, vmt)


def _diff_attn_body(sc_ref, q_ref, k_ref, vt_ref, lq1_ref, lk1_ref, lq2_ref, lk2_ref, g_ref, o_ref,
                    m_ref, acc_ref, qq_ref, t0_ref, t1_ref, *, blk):
    l_ref = None
    bq = 2 * blk
    h = pl.program_id(1)
    qi = pl.program_id(2)
    slope = sc_ref[h]
    lam_init = sc_ref[DIFF_HEADS]
    q = q_ref[0]
    lo = lax.broadcasted_iota(jnp.int32, q.shape, 1) < DIFF_D
    zero = jnp.zeros(q.shape, q.dtype)
    qq_ref[0:bq, :] = jnp.where(lo, q, zero)
    qq_ref[bq:2 * bq, :] = jnp.where(lo, zero, q)
    _init_stats(m_ref, l_ref, acc_ref)
    key = lax.broadcasted_iota(jnp.int32, (blk, 2 * bq), 0)
    col = lax.broadcasted_iota(jnp.int32, (blk, 2 * bq), 1)
    qry = jnp.where(col >= bq, col - bq, col)

    def keys(kj):
        return pl.ds(pl.multiple_of(kj * blk, blk), blk)

    def scores(kj):
        return _dot_nt(k_ref[0, keys(kj), :], qq_ref[...]) + slope * key.astype(F32)

    def consume(t_ref, kj, masked):
        t = t_ref[...]
        if masked:
            t = jnp.where(key + (kj - 2 * qi) * blk <= qry, t, NEG_INF)
        shift = slope * ((kj - 2 * qi) * blk).astype(F32)
        _online_update_t(m_ref, l_ref, acc_ref, t, shift, vt_ref[:, keys(kj)])

    _causal_sweep(qi, scores, consume, t0_ref, t1_ref)

    lam = (jnp.exp(jnp.sum(lq1_ref[...] * lk1_ref[...], axis=-1, keepdims=True))
           - jnp.exp(jnp.sum(lq2_ref[...] * lk2_ref[...], axis=-1, keepdims=True)) + lam_init)
    acc = acc_ref[...]
    on = acc[:DIFF_V] * (1.0 / acc[DIFF_V:DIFF_V + 1])
    o = on[:, :bq] - lam * on[:, bq:]
    o = o * lax.rsqrt(jnp.sum(o * o, axis=0, keepdims=True) * (1.0 / DIFF_V) + EPS)
    o_ref[0] = ((o * g_ref[...]) * (1.0 - lam_init)).T.astype(o_ref.dtype)


def _diff_attn(sc, qd, kd, vdt, lq1, lk1, lq2, lk2, gout, l, *, blk):
    B, S, _ = qd.shape
    bq = 2 * blk
    est = 2 * (2 * bq * 128 * 2 + 2 * S * 128 * 2) + 4 * bq * 128 * 4 + 6 * blk * 2 * bq * 4
    return pl.pallas_call(
        functools.partial(_diff_attn_body, blk=blk),
        grid=(B, DIFF_HEADS, S // bq),
        in_specs=[pl.BlockSpec(memory_space=pltpu.SMEM),
                  pl.BlockSpec((1, bq, 128), lambda b, h, i: (b, i, h)),
                  pl.BlockSpec((1, S, 128), lambda b, h, i: (b, 0, h)),
                  pl.BlockSpec((VT_ROWS, S), lambda b, h, i: (h, b)),
                  _layer_vec(lq1, l), _layer_vec(lk1, l), _layer_vec(lq2, l), _layer_vec(lk2, l),
                  _layer_vec(gout, l)],
        out_specs=pl.BlockSpec((1, bq, 128), lambda b, h, i: (b, i, h)),
        out_shape=jax.ShapeDtypeStruct((B, S, DIFF_HEADS * DIFF_V), BF16),
        scratch_shapes=[pltpu.VMEM((1, 2 * bq), F32),
                        pltpu.VMEM((VT_ROWS, 2 * bq), F32), pltpu.VMEM((2 * bq, 128), BF16),
                        pltpu.VMEM((blk, 2 * bq), F32), pltpu.VMEM((blk, 2 * bq), F32)],
        compiler_params=pltpu.CompilerParams(
            dimension_semantics=("arbitrary", "arbitrary", "arbitrary"), vmem_limit_bytes=_vmem_limit(est)),
        name="diff_attn",
    )(sc, qd, kd, vdt, lq1, lk1, lq2, lk2, gout)


def _dil_attn_body(sc_ref, q_ref, k_ref, v_ref, o_ref, qs_ref, ks_ref, vs_ref, op_ref, lse_ref, *, S):
    h = pl.program_id(1)
    slope = sc_ref[h]
    blk = DIL_BLOCK
    ii = lax.broadcasted_iota(jnp.int32, (blk, 2 * blk), 0)
    jj = lax.broadcasted_iota(jnp.int32, (blk, 2 * blk), 1)
    back = ii + blk - jj
    in_band = (back >= 0) & (back <= blk)
    ones = jnp.ones((blk, V7X_LANES), BF16)

    for p, (window, r) in enumerate(DIL_PATTERNS):
        assert window // r == blk
        L = S // r
        nb = L // blk
        Lp = L + blk
        bias = jnp.where(in_band, (-slope * float(r)) * back.astype(F32), NEG_INF)

        def src_rows(start, r=r):
            return pl.ds(start, blk, stride=r) if r > 1 else pl.ds(start, blk)

        def zero_pad(c, carry, Lp=Lp):
            pad = pl.ds(pl.multiple_of(c * Lp, blk), blk)
            ks_ref[pad, :] = jnp.zeros((blk, V7X_LANES), BF16)
            vs_ref[pad, :] = jnp.zeros((blk, 2 * V7X_LANES), BF16)
            return carry

        lax.fori_loop(0, r, zero_pad, 0)

        def gather(idx, carry, r=r, nb=nb, L=L, Lp=Lp, src_rows=src_rows):
            c = idx // nb
            n = idx % nb
            src = src_rows(c + n * (blk * r))
            dst = pl.ds(pl.multiple_of(c * Lp + (n + 1) * blk, blk), blk)
            qs_ref[pl.ds(pl.multiple_of(c * L + n * blk, blk), blk), :] = q_ref[0, src, :].astype(BF16)
            ks_ref[dst, :] = k_ref[0, src, :].astype(BF16)
            vs_ref[dst, :] = jnp.concatenate([v_ref[0, src, :].astype(BF16), ones], axis=1)
            return carry

        lax.fori_loop(0, r * nb, gather, 0)

        def blocks(it, carry, r=r, nb=nb, L=L, Lp=Lp, p=p, bias=bias, src_rows=src_rows):
            for u in range(DIL_UNROLL):
                idx = it * DIL_UNROLL + u
                c = idx // nb
                n = idx % nb
                q = qs_ref[pl.ds(pl.multiple_of(c * L + n * blk, blk), blk), :]
                keys = pl.ds(pl.multiple_of(c * Lp + n * blk, blk), 2 * blk)
                s = _dot_nt(q, ks_ref[keys, :]) + bias
                s = jnp.where((n >= 1) | (jj >= blk), s, NEG_INF)
                m = jnp.max(s, axis=-1, keepdims=True)
                e = jnp.exp2(s - m)
                ov = _dot(e.astype(BF16), vs_ref[keys, :])
                den = ov[:, V7X_LANES:]
                dst = src_rows(c + n * (blk * r))
                op_ref[p, dst, :] = ov[:, :V7X_LANES] * (1.0 / den)
                lse_ref[p, dst, :] = m + jnp.log2(den)
            return carry

        lax.fori_loop(0, (r * nb) // DIL_UNROLL, blocks, 0)

    chunk = 256

    def combine(ci, carry):
        rows = pl.ds(pl.multiple_of(ci * chunk, chunk), chunk)
        l0 = lse_ref[0, rows, :]
        l1 = lse_ref[1, rows, :]
        l2 = lse_ref[2, rows, :]
        mm = jnp.maximum(jnp.maximum(l0, l1), l2)
        w0 = jnp.exp2(l0 - mm)
        w1 = jnp.exp2(l1 - mm)
        w2 = jnp.exp2(l2 - mm)
        num = w0 * op_ref[0, rows, :] + w1 * op_ref[1, rows, :] + w2 * op_ref[2, rows, :]
        o_ref[0, rows, :] = (num * (1.0 / (w0 + w1 + w2))).astype(o_ref.dtype)
        return carry

    lax.fori_loop(0, S // chunk, combine, 0)


def _dil_attn(sc, ql, kl, vl):
    B, S, _ = ql.shape
    r_max = max(r for _, r in DIL_PATTERNS)
    assert S % (DIL_BLOCK * r_max) == 0 and (S // DIL_BLOCK) % DIL_UNROLL == 0
    assert len(DIL_PATTERNS) == 3
    pad_rows = S + DIL_BLOCK * r_max
    est = (2 * (3 * S * 128 * 4 + S * 128 * 2) + 6 * S * 128 * 4
           + S * 128 * 2 + pad_rows * 128 * 2 + pad_rows * 256 * 2)
    return pl.pallas_call(
        functools.partial(_dil_attn_body, S=S),
        grid=(B, DIL_HEADS),
        in_specs=[pl.BlockSpec(memory_space=pltpu.SMEM),
                  pl.BlockSpec((1, S, 128), lambda b, h: (b, 0, h)),
                  pl.BlockSpec((1, S, 128), lambda b, h: (b, 0, h)),
                  pl.BlockSpec((1, S, 128), lambda b, h: (b, 0, h))],
        out_specs=pl.BlockSpec((1, S, 128), lambda b, h: (b, 0, h)),
        out_shape=jax.ShapeDtypeStruct((B, S, DIL_HEADS * DIL_D), BF16),
        scratch_shapes=[pltpu.VMEM((S, 128), BF16), pltpu.VMEM((pad_rows, 128), BF16),
                        pltpu.VMEM((pad_rows, 256), BF16),
                        pltpu.VMEM((3, S, 128), F32), pltpu.VMEM((3, S, 128), F32)],
        compiler_params=pltpu.CompilerParams(
            dimension_semantics=("arbitrary", "arbitrary"), vmem_limit_bytes=_vmem_limit(est)),
        name="dil_attn",
    )(sc, ql, kl, vl)


def _merge_body(x_ref, gz_ref, gb_ref, oa_ref, ob_ref, oc_ref, wa_ref, wb_ref, wc_ref, wo_ref, o_ref):
    D = x_ref.shape[-1]

    def gate(b):
        return jax.nn.sigmoid(gz_ref[:, b * D:(b + 1) * D].astype(F32) + gb_ref[:, b * D:(b + 1) * D])

    merged = (gate(0) * _dot(oa_ref[...], wa_ref[...])
              + gate(1) * _dot(ob_ref[...], wb_ref[...])
              + gate(2) * _dot(oc_ref[...], wc_ref[...]))
    o_ref[...] = x_ref[...] + _dot(merged.astype(BF16), wo_ref[...])


def _merge(x, z, gate_bias, oa, ob, oc, wa, wb, wc, wo, l, *, tm):
    T, D = x.shape

    def resident(a):
        return pl.BlockSpec((None,) + a.shape[1:], lambda i: (l, 0, 0), pipeline_mode=pl.Buffered(1))

    def rows(width):
        return pl.BlockSpec((tm, width), lambda i: (i, 0))

    zsz = jnp.dtype(z.dtype).itemsize
    est = (2 * tm * D * 4 * 2 + 2 * tm * N_BRANCH * D * zsz + 2 * 3 * tm * HW * 2
           + (3 * HW * D + D * D) * 2 + N_BRANCH * D * 4 + 4 * tm * D * 4)
    return pl.pallas_call(
        _merge_body,
        grid=(T // tm,),
        in_specs=[rows(D), rows(N_BRANCH * D), resident(gate_bias), rows(HW), rows(HW), rows(HW),
                  resident(wa), resident(wb), resident(wc), resident(wo)],
        out_specs=rows(D),
        out_shape=jax.ShapeDtypeStruct((T, D), F32),
        compiler_params=pltpu.CompilerParams(
            dimension_semantics=("arbitrary",), vmem_limit_bytes=_vmem_limit(est)),
        name="merge",
    )(x, z, gate_bias, oa, ob, oc, wa, wb, wc, wo)


def _ffn_up_body(x_ref, g_ref, wg_ref, wu_ref, cwg_ref, cwu_ref, cbg_ref, cbu_ref, o_ref, h_ref, *, tm, tiles_per_seq):
    i = pl.program_id(0)

    @pl.when(pl.program_id(1) == 0)
    def _():
        @pl.when(i % tiles_per_seq == 0)
        def _():
            h_ref[0:HALO, :] = jnp.zeros((HALO, h_ref.shape[1]), BF16)

        @pl.when(i % tiles_per_seq != 0)
        def _():
            h_ref[0:HALO, :] = h_ref[tm:tm + HALO, :]

        x = x_ref[...]
        h_ref[HALO:HALO + tm, :] = (_rms(x, x.shape[-1]) * g_ref[...]).astype(BF16)

    h = h_ref[...]

    def conv(u, cw_ref, cb_ref):
        return (u[HALO:HALO + tm] * cw_ref[2:3, :] + u[HALO - 1:HALO - 1 + tm] * cw_ref[1:2, :]
                + u[HALO - 2:HALO - 2 + tm] * cw_ref[0:1, :] + cb_ref[...])

    gate = conv(_dot(h, wg_ref[...]), cwg_ref, cbg_ref)
    up = conv(_dot(h, wu_ref[...]), cwu_ref, cbu_ref)
    o_ref[...] = (gate * jax.nn.sigmoid(gate) * up).astype(o_ref.dtype)


def _ffn_up(x, g, w_up, conv_w, conv_b, l, *, S, tm, tn):
    T, K = x.shape
    F = w_up.shape[2] // 2
    nj = F // tn
    est = 2 * tm * K * 4 + (tm + HALO) * K * 2 + 2 * 2 * K * tn * 2 + 2 * tm * tn * 2 + 6 * (tm + HALO) * tn * 4
    return pl.pallas_call(
        functools.partial(_ffn_up_body, tm=tm, tiles_per_seq=S // tm),
        grid=(T // tm, nj),
        in_specs=[pl.BlockSpec((tm, K), lambda i, j: (i, 0)),
                  _layer_vec(g, l),
                  pl.BlockSpec((None, K, tn), lambda i, j: (l, 0, j)),
                  pl.BlockSpec((None, K, tn), lambda i, j: (l, 0, j + nj)),
                  pl.BlockSpec((None, CONV_W, tn), lambda i, j: (l, 0, j)),
                  pl.BlockSpec((None, CONV_W, tn), lambda i, j: (l, 0, j + nj)),
                  pl.BlockSpec((None, 1, tn), lambda i, j: (l, 0, j)),
                  pl.BlockSpec((None, 1, tn), lambda i, j: (l, 0, j + nj))],
        out_specs=pl.BlockSpec((tm, tn), lambda i, j: (i, j)),
        out_shape=jax.ShapeDtypeStruct((T, F), BF16),
        scratch_shapes=[pltpu.VMEM((tm + HALO, K), BF16)],
        compiler_params=pltpu.CompilerParams(
            dimension_semantics=("arbitrary", "arbitrary"), vmem_limit_bytes=_vmem_limit(est)),
        name="ffn_up",
    )(x, g, w_up, w_up, conv_w, conv_w, conv_b, conv_b)


def _ffn_down_body(x_ref, a_ref, w_ref, o_ref):
    o_ref[...] = x_ref[...] + _dot(a_ref[...], w_ref[...])


def _ffn_down(x, act, w, l, *, tm):
    T, D = x.shape
    F = act.shape[1]
    est = 2 * (2 * tm * D * 4 + tm * F * 2) + F * D * 2
    return pl.pallas_call(
        _ffn_down_body,
        grid=(T // tm,),
        in_specs=[pl.BlockSpec((tm, D), lambda i: (i, 0)),
                  pl.BlockSpec((tm, F), lambda i: (i, 0)),
                  pl.BlockSpec((None, F, D), lambda i: (l, 0, 0), pipeline_mode=pl.Buffered(1))],
        out_specs=pl.BlockSpec((tm, D), lambda i: (i, 0)),
        out_shape=jax.ShapeDtypeStruct((T, D), F32),
        compiler_params=pltpu.CompilerParams(
            dimension_semantics=("arbitrary",), vmem_limit_bytes=_vmem_limit(est)),
        name="ffn_down",
    )(x, act, w)


def _rope_lanes(t):
    half = MLA_ROPE // 2
    z = jnp.zeros(t.shape[:-1] + (half,), t.dtype)
    return jnp.concatenate([t[..., :half], z, t[..., half:], z], axis=-1)


def _z_columns(D):
    col = {"gz": 0}
    c = N_BRANCH * D
    for name in ("dq", "dk", "dv", "lq", "lk", "lv"):
        col[name] = c
        c += HW
    col["cq"] = c
    col["ckv"] = c + 512
    col["kr"] = c + 1024
    return col, c + 1024 + 128


def _w_in_layout_body(w_ref, o_ref, *, D):
    o_cq, o_kr = 0, 1024
    o_d = o_kr + MLA_ROPE
    o_gz = o_d + 6 * HW
    n_gz = N_BRANCH * D
    col, used = _z_columns(D)
    tk = w_ref.shape[0]
    o_ref[:, col["gz"]:col["gz"] + n_gz] = w_ref[:, o_gz:o_gz + n_gz].astype(BF16)
    o_ref[:, col["dq"]:col["dq"] + 6 * HW] = w_ref[:, o_d:o_gz].astype(BF16)
    o_ref[:, col["cq"]:col["cq"] + o_kr] = w_ref[:, o_cq:o_kr].astype(BF16)
    half = MLA_ROPE // 2
    z = jnp.zeros((tk, half), BF16)
    o_ref[:, col["kr"]:col["kr"] + 128] = jnp.concatenate(
        [w_ref[:, o_kr:o_kr + half].astype(BF16), z, w_ref[:, o_kr + half:o_d].astype(BF16), z], axis=1)
    o_ref[:, used:] = jnp.zeros((tk, o_ref.shape[1] - used), BF16)


def _w_in_layout(w, D, n_pad, *, tk):
    depth, K, N = w.shape
    return pl.pallas_call(
        functools.partial(_w_in_layout_body, D=D),
        grid=(depth, K // tk),
        in_specs=[pl.BlockSpec((None, tk, N), lambda l, i: (l, i, 0))],
        out_specs=pl.BlockSpec((None, tk, n_pad), lambda l, i: (l, i, 0)),
        out_shape=jax.ShapeDtypeStruct((depth, K, n_pad), BF16),
        compiler_params=pltpu.CompilerParams(
            dimension_semantics=("arbitrary", "arbitrary"),
            vmem_limit_bytes=_vmem_limit(2 * tk * (N * 4 + n_pad * 2) + tk * N * 4)),
        name="w_in_layout",
    )(w)


def _w_uq_layout(w):
    r = w.reshape(w.shape[:-1] + (MLA_HEADS, MLA_NOPE + MLA_ROPE))
    out = jnp.concatenate([r[..., :MLA_NOPE], _rope_lanes(r[..., MLA_NOPE:])], axis=-1)
    return out.reshape(w.shape[:-1] + (MLA_HEADS * 2 * V7X_LANES,)).astype(BF16)


def _rope_tables(S):
    half = MLA_ROPE // 2
    inv = ROPE_THETA ** (-jnp.arange(half, dtype=F32) / half)
    ang = jnp.arange(S, dtype=F32)[:, None] * inv[None, :]
    cos, sin = jnp.cos(ang), jnp.sin(ang)
    z = jnp.zeros_like(cos)
    return jnp.concatenate([cos, z, cos, z], axis=-1), jnp.concatenate([-sin, z, sin, z], axis=-1)


def kernel(x, attn_norm_g, w_in, mla_q_lora_norm_g, mla_kv_lora_norm_g, mla_w_uq, mla_w_uk, mla_w_uv, mla_q_gain, mla_k_gain, diff_q_gain, diff_k_gain, diff_lam_q1, diff_lam_k1, diff_lam_q2, diff_lam_k2, diff_out_norm_g, dil_q_gain, dil_k_gain, gate_bias, w_br_mla, w_br_diff, w_br_dil, w_out, ffn_norm_g, ffn_w_up, ffn_conv_w, ffn_conv_b, ffn_w_down):
    B, S, D = x.shape
    depth = w_in.shape[0]
    T = B * S
    col, n_used = _z_columns(D)
    n_z = -(-n_used // 1024) * 1024
    cos_t, sin_t = _rope_tables(S)
    slopes = LOG2E * 2.0 ** (-8.0 * jnp.arange(1, DIFF_HEADS + DIL_HEADS + 1, dtype=F32) / (DIFF_HEADS + DIL_HEADS))
    s_mla = LOG2E * (MLA_NOPE + MLA_ROPE) ** -0.5
    rows = lambda v: v.reshape(depth, 1, -1).astype(F32)

    w_in_p = _w_in_layout(w_in, D, n_z, tk=128)
    w_uq_p = _w_uq_layout(mla_w_uq)
    w_uk_p = mla_w_uk.astype(BF16)
    w_uvt_p = jnp.swapaxes(mla_w_uv, 1, 2).astype(BF16)
    w_bra, w_brb, w_brc, w_out_p = (w.astype(BF16) for w in (w_br_mla, w_br_diff, w_br_dil, w_out))
    w_up_p, w_down_p = ffn_w_up.astype(BF16), ffn_w_down.astype(BF16)
    vecs = {
        "gql": rows(mla_q_lora_norm_g), "gkvl": rows(mla_kv_lora_norm_g),
        "gqn": rows(mla_q_gain[:, :MLA_NOPE] * s_mla), "gqr": rows(_rope_lanes(mla_q_gain[:, MLA_NOPE:]) * s_mla),
        "gkn": rows(mla_k_gain[:, :MLA_NOPE]), "gkr": rows(_rope_lanes(mla_k_gain[:, MLA_NOPE:])),
        "gdq": rows(jnp.tile(diff_q_gain, (1, 2)) * (LOG2E * DIFF_D ** -0.5)), "gdk": rows(jnp.tile(diff_k_gain, (1, 2))),
        "glq": rows(dil_q_gain * (LOG2E * DIL_D ** -0.5)), "glk": rows(dil_k_gain),
    }
    g_attn, g_ffn, g_bias, c_b = rows(attn_norm_g), rows(ffn_norm_g), rows(gate_bias), rows(ffn_conv_b)
    lam_q1, lam_k1, lam_q2, lam_k2 = rows(diff_lam_q1), rows(diff_lam_k1), rows(diff_lam_q2), rows(diff_lam_k2)
    g_dout = diff_out_norm_g.reshape(depth, DIFF_V, 1).astype(F32)
    sc_dil = jnp.concatenate([slopes[1::2], jnp.zeros((2,), F32)])

    xf = x.reshape(T, D)
    b3 = lambda a: a.reshape(B, S, a.shape[-1])
    flat = lambda a: a.reshape(T, a.shape[-1])
    for l in range(depth):
        lam_init = 0.8 - 0.6 * math.exp(-0.3 * l)
        z = _norm_matmul(xf, g_attn, w_in_p, l, tm=1024, tn=1024, out_dtype=BF16)
        qm, km, vmt, qd, kd, vdt, ql, kl, vl = _prep(
            z, cos_t, sin_t, vecs, w_uq_p, w_uk_p, w_uvt_p, l, S=S, tm=256, col=col)
        o_mla = _mla_attn(b3(qm), b3(km), vmt, blk=512)
        sc_diff = jnp.concatenate([slopes[0::2], jnp.full((2,), lam_init, F32)])
        o_diff = _diff_attn(sc_diff, b3(qd), b3(kd), vdt, lam_q1, lam_k1, lam_q2, lam_k2, g_dout, l, blk=512)
        o_dil = _dil_attn(sc_dil, b3(ql), b3(kl), b3(vl))
        xf = _merge(xf, z, g_bias, flat(o_mla), flat(o_diff), flat(o_dil), w_bra, w_brb, w_brc, w_out_p, l, tm=256)
        act = _ffn_up(xf, g_ffn, w_up_p, ffn_conv_w, c_b, l, S=S, tm=1024, tn=512)
        xf = _ffn_down(xf, act, w_down_p, l, tm=256)
    return xf.reshape(B, S, D)
```

```python
import functools
import math

import jax
import jax.numpy as jnp
from jax import lax
from jax.experimental import pallas as pl
from jax.experimental.pallas import tpu as pltpu

F32 = jnp.float32
BF16 = jnp.bfloat16
EPS = 1e-6
NEG_INF = float("-inf")
LOG2E = math.log2(math.e)

V7X_LANES = 128
V7X_VMEM_BYTES = 64 * 1024 * 1024
BF16_ROWS_PER_VREG = 16

MLA_HEADS = 6
MLA_NOPE = 128
MLA_ROPE = 64
MLA_V = 128
ROPE_THETA = 10000.0
DIFF_HEADS = 6
DIFF_D = 64
DIFF_V = 128
DIL_HEADS = 6
DIL_D = 128
DIL_PATTERNS = ((128, 1), (512, 4), (2048, 16))
DIL_BLOCK = 128
DIL_UNROLL = 16
N_BRANCH = 3
CONV_W = 3
HALO = BF16_ROWS_PER_VREG
HW = 6 * 128
VT_ROWS = DIFF_V + BF16_ROWS_PER_VREG


def _vmem_limit(block_bytes):
    return int(min(V7X_VMEM_BYTES - (4 << 20), block_bytes + (16 << 20)))


def _dot(a, b):
    return jnp.dot(a, b, preferred_element_type=F32)


def _dot_nt(a, b):
    return lax.dot_general(a, b, (((1,), (1,)), ((), ())), preferred_element_type=F32)


def _rms(x, n):
    return x * lax.rsqrt(jnp.sum(x * x, axis=-1, keepdims=True) * (1.0 / n) + EPS)


def _layer_vec(a, l):
    return pl.BlockSpec((None,) + a.shape[1:], lambda *_: (l, 0, 0))


def _norm_matmul_body(x_ref, g_ref, w_ref, o_ref, h_ref):
    @pl.when(pl.program_id(1) == 0)
    def _():
        x = x_ref[...]
        h_ref[...] = (_rms(x, x.shape[-1]) * g_ref[...]).astype(BF16)

    o_ref[...] = _dot(h_ref[...], w_ref[...]).astype(o_ref.dtype)


def _norm_matmul(x, g, w, l, *, tm, tn, out_dtype):
    T, K = x.shape
    N = w.shape[2]
    osz = jnp.dtype(out_dtype).itemsize
    est = 2 * tm * K * 4 + tm * K * 2 + 2 * K * tn * 2 + 2 * tm * tn * osz
    return pl.pallas_call(
        _norm_matmul_body,
        grid=(T // tm, N // tn),
        in_specs=[pl.BlockSpec((tm, K), lambda i, j: (i, 0)),
                  _layer_vec(g, l),
                  pl.BlockSpec((None, K, tn), lambda i, j: (l, 0, j))],
        out_specs=pl.BlockSpec((tm, tn), lambda i, j: (i, j)),
        out_shape=jax.ShapeDtypeStruct((T, N), out_dtype),
        scratch_shapes=[pltpu.VMEM((tm, K), BF16)],
        compiler_params=pltpu.CompilerParams(
            dimension_semantics=("arbitrary", "arbitrary"), vmem_limit_bytes=_vmem_limit(est)),
        name="norm_matmul",
    )(x, g, w)


def _rope(xb, c, s):
    return xb * c + pltpu.roll(xb, 64, 1) * s


def _prep_body(dq_ref, dk_ref, dv_ref, lq_ref, lk_ref, lv_ref, cq_ref, ckv_ref, kr_ref,
               cos_ref, sin_ref, gql_ref, gkvl_ref, wuq_ref, wuk_ref, wuvt_ref,
               gqn_ref, gqr_ref, gkn_ref, gkr_ref, gdq_ref, gdk_ref, glq_ref, glk_ref,
               qm_ref, km_ref, vmt_ref, qd_ref, kd_ref, vdt_ref, ql_ref, kl_ref, vl_ref):
    c = cos_ref[...]
    s = sin_ref[...]
    tm = c.shape[0]
    f32 = lambda ref: ref[...].astype(F32)

    cqn = (_rms(f32(cq_ref), cq_ref.shape[-1]) * gql_ref[...]).astype(BF16)
    q = _dot(cqn, wuq_ref[...])
    for h in range(MLA_HEADS):
        a = 2 * V7X_LANES * h
        qm_ref[:, a:a + 128] = (_rms(q[:, a:a + 128], MLA_NOPE) * gqn_ref[...]).astype(BF16)
        qr = _rms(q[:, a + 128:a + 256], MLA_ROPE) * gqr_ref[...]
        qm_ref[:, a + 128:a + 256] = _rope(qr, c, s).astype(BF16)
    ckvn = (_rms(f32(ckv_ref), ckv_ref.shape[-1]) * gkvl_ref[...]).astype(BF16)
    kn = _dot(ckvn, wuk_ref[...])
    kr = _rope(_rms(f32(kr_ref), MLA_ROPE) * gkr_ref[...], c, s).astype(BF16)
    for h in range(MLA_HEADS):
        a = V7X_LANES * h
        km_ref[:, 2 * a:2 * a + 128] = (_rms(kn[:, a:a + 128], MLA_NOPE) * gkn_ref[...]).astype(BF16)
        km_ref[:, 2 * a + 128:2 * a + 256] = kr
    vmt_ref[...] = _dot_nt(wuvt_ref[...], ckvn).astype(BF16)

    lo = lax.broadcasted_iota(jnp.int32, (tm, V7X_LANES), 1) < DIFF_D

    def half_rms(xh):
        sq = xh * xh
        s_lo = jnp.sum(jnp.where(lo, sq, 0.0), axis=-1, keepdims=True)
        s_hi = jnp.sum(jnp.where(lo, 0.0, sq), axis=-1, keepdims=True)
        inv = jnp.where(lo, lax.rsqrt(s_lo * (1.0 / DIFF_D) + EPS), lax.rsqrt(s_hi * (1.0 / DIFF_D) + EPS))
        return xh * inv

    for h in range(DIFF_HEADS):
        a = V7X_LANES * h
        qd_ref[:, a:a + 128] = (half_rms(dq_ref[:, a:a + 128].astype(F32)) * gdq_ref[...]).astype(BF16)
        kd_ref[:, a:a + 128] = (half_rms(dk_ref[:, a:a + 128].astype(F32)) * gdk_ref[...]).astype(BF16)
    vdt = f32(dv_ref).T.astype(BF16)
    for h in range(DIFF_HEADS):
        vdt_ref[VT_ROWS * h:VT_ROWS * h + DIFF_V, :] = vdt[DIFF_V * h:DIFF_V * (h + 1)]
        vdt_ref[VT_ROWS * h + DIFF_V:VT_ROWS * (h + 1), :] = jnp.ones((VT_ROWS - DIFF_V, tm), BF16)

    for h in range(DIL_HEADS):
        a = V7X_LANES * h
        ql_ref[:, a:a + 128] = _rms(lq_ref[:, a:a + 128].astype(F32), DIL_D) * glq_ref[...]
        kl_ref[:, a:a + 128] = _rms(lk_ref[:, a:a + 128].astype(F32), DIL_D) * glk_ref[...]
    vl_ref[...] = f32(lv_ref)


def _prep(z, cos_t, sin_t, vecs, wuq, wuk, wuvt, l, *, S, tm, col):
    T = z.shape[0]
    nseq = S // tm

    def zspec(width, start):
        assert start % width == 0
        return pl.BlockSpec((tm, width), lambda i, blk=start // width: (i, blk))

    def wspec(a):
        return pl.BlockSpec((None,) + a.shape[1:], lambda i: (l, 0, 0))

    in_specs = [zspec(HW, col["dq"]), zspec(HW, col["dk"]), zspec(HW, col["dv"]),
                zspec(HW, col["lq"]), zspec(HW, col["lk"]), zspec(HW, col["lv"]),
                zspec(512, col["cq"]), zspec(512, col["ckv"]), zspec(128, col["kr"]),
                pl.BlockSpec((tm, 128), lambda i: (i % nseq, 0)),
                pl.BlockSpec((tm, 128), lambda i: (i % nseq, 0))]
    names = ["gql", "gkvl"]
    names2 = ["gqn", "gqr", "gkn", "gkr", "gdq", "gdk", "glq", "glk"]
    in_specs += [_layer_vec(vecs[n], l) for n in names] + [wspec(wuq), wspec(wuk), wspec(wuvt)]
    in_specs += [_layer_vec(vecs[n], l) for n in names2]

    def ospec(width):
        return pl.BlockSpec((tm, width), lambda i: (i, 0))

    tspec = pl.BlockSpec((HW, tm), lambda i: (0, i))
    out_shape = [jax.ShapeDtypeStruct((T, 2 * HW), BF16),
                 jax.ShapeDtypeStruct((T, 2 * HW), BF16),
                 jax.ShapeDtypeStruct((HW, T), BF16),
                 jax.ShapeDtypeStruct((T, HW), BF16),
                 jax.ShapeDtypeStruct((T, HW), BF16),
                 jax.ShapeDtypeStruct((DIFF_HEADS * VT_ROWS, T), BF16),
                 jax.ShapeDtypeStruct((T, HW), F32),
                 jax.ShapeDtypeStruct((T, HW), F32),
                 jax.ShapeDtypeStruct((T, HW), F32)]
    vdspec = pl.BlockSpec((DIFF_HEADS * VT_ROWS, tm), lambda i: (0, i))
    out_specs = [ospec(2 * HW), ospec(2 * HW), tspec, ospec(HW), ospec(HW), vdspec, ospec(HW), ospec(HW), ospec(HW)]
    est = 2 * tm * (6 * HW + 1024 + 128 + 256) * 4 + 2 * 2 * 512 * 1536 * 2 + 2 * tm * (12 * HW) * 4
    return pl.pallas_call(
        _prep_body,
        grid=(T // tm,),
        in_specs=in_specs,
        out_specs=out_specs,
        out_shape=out_shape,
        compiler_params=pltpu.CompilerParams(
            dimension_semantics=("arbitrary",), vmem_limit_bytes=_vmem_limit(est)),
        name="prep",
    )(z, z, z, z, z, z, z, z, z, cos_t, sin_t, vecs["gql"], vecs["gkvl"], wuq, wuk, wuvt,
      *[vecs[n] for n in names2])


def _online_update_t(m_ref, l_ref, acc_ref, t, shift, vt):
    m_prev = m_ref[...]
    m_new = jnp.maximum(m_prev, jnp.max(t, axis=0, keepdims=True) + shift)
    alpha = jnp.exp2(m_prev - m_new)
    p = jnp.exp2(t + (shift - m_new))
    if l_ref is not None:
        l_ref[...] = alpha * l_ref[...] + jnp.sum(p, axis=0, keepdims=True)
    acc_ref[...] = alpha * acc_ref[...] + _dot(vt, p.astype(BF16))
    m_ref[...] = m_new


def _init_stats(m_ref, l_ref, acc_ref):
    m_ref[...] = jnp.full(m_ref.shape, NEG_INF, F32)
    if l_ref is not None:
        l_ref[...] = jnp.zeros(l_ref.shape, F32)
    acc_ref[...] = jnp.zeros(acc_ref.shape, F32)


def _causal_sweep(qi, scores, consume, t0_ref, t1_ref, last):
    t0_ref[...] = scores(0)

    def pair(i, carry):
        t1_ref[...] = scores(2 * i + 1)
        consume(t0_ref, 2 * i, False)
        t0_ref[...] = scores(2 * i + 2)
        consume(t1_ref, 2 * i + 1, False)
        return carry

    lax.fori_loop(0, qi, pair, 0)
    last_scores, last_consume = last
    last_scores(t1_ref)
    consume(t0_ref, 2 * qi, True)
    last_consume(t1_ref)


def _mla_attn_body(q_ref, k_ref, vt_ref, o_ref, m_ref, l_ref, acc_ref, t0_ref, t1_ref, *, blk):
    bq = 2 * blk

    def keys(kj):
        return pl.ds(pl.multiple_of(kj * blk, blk), blk)

    def q_block(qi, carry):
        q_rows = pl.ds(pl.multiple_of(qi * bq, bq), bq)
        late_rows = pl.ds(pl.multiple_of(qi * bq + blk, blk), blk)
        _init_stats(m_ref, l_ref, acc_ref)

        def scores(kj):
            return _dot_nt(k_ref[0, keys(kj), :], q_ref[0, q_rows, :])

        def consume(t_ref, kj, masked):
            t = t_ref[...]
            if masked:
                key = lax.broadcasted_iota(jnp.int32, t.shape, 0) + (kj - 2 * qi) * blk
                qry = lax.broadcasted_iota(jnp.int32, t.shape, 1)
                t = jnp.where(key <= qry, t, NEG_INF)
            _online_update_t(m_ref, l_ref, acc_ref, t, 0.0, vt_ref[:, keys(kj)])

        def last_scores(t_ref):
            t_ref[:, blk:] = _dot_nt(k_ref[0, keys(2 * qi + 1), :], q_ref[0, late_rows, :])

        def last_consume(t_ref):
            t = t_ref[:, blk:]
            key = lax.broadcasted_iota(jnp.int32, t.shape, 0)
            qry = lax.broadcasted_iota(jnp.int32, t.shape, 1)
            t = jnp.where(key <= qry, t, NEG_INF)
            _online_update_t(m_ref.at[:, blk:], l_ref.at[:, blk:], acc_ref.at[:, blk:], t, 0.0,
                             vt_ref[:, keys(2 * qi + 1)])

        _causal_sweep(qi, scores, consume, t0_ref, t1_ref, (last_scores, last_consume))
        o_ref[0, q_rows, :] = (acc_ref[...] * (1.0 / l_ref[...])).T.astype(o_ref.dtype)
        return carry

    lax.fori_loop(0, q_ref.shape[1] // bq, q_block, 0)


def _mla_attn(qm, km, vmt, *, blk):
    B, S, _ = qm.shape
    bq = 2 * blk
    est = 2 * (2 * S * 256 * 2 + 2 * S * 128 * 2) + 3 * bq * 128 * 4 + 6 * blk * bq * 4
    return pl.pallas_call(
        functools.partial(_mla_attn_body, blk=blk),
        grid=(B, MLA_HEADS),
        in_specs=[pl.BlockSpec((1, S, 256), lambda b, h: (b, 0, h)),
                  pl.BlockSpec((1, S, 256), lambda b, h: (b, 0, h)),
                  pl.BlockSpec((128, S), lambda b, h: (h, b))],
        out_specs=pl.BlockSpec((1, S, 128), lambda b, h: (b, 0, h)),
        out_shape=jax.ShapeDtypeStruct((B, S, MLA_HEADS * MLA_V), BF16),
        scratch_shapes=[pltpu.VMEM((1, bq), F32), pltpu.VMEM((1, bq), F32), pltpu.VMEM((128, bq), F32),
                        pltpu.VMEM((blk, bq), F32), pltpu.VMEM((blk, bq), F32)],
        compiler_params=pltpu.CompilerParams(
            dimension_semantics=("arbitrary", "arbitrary"), vmem_limit_bytes=_vmem_limit(est)),
        name="mla_attn",
    )(qm, km, vmt)


def _diff_attn_body(sc_ref, q_ref, k_ref, vt_ref, lq1_ref, lk1_ref, lq2_ref, lk2_ref, g_ref, o_ref,
                    m_ref, acc_ref, qq_ref, t0_ref, t1_ref, *, blk):
    l_ref = None
    bq = 2 * blk
    h = pl.program_id(1)
    slope = sc_ref[h]
    lam_init = sc_ref[DIFF_HEADS]
    lo = lax.broadcasted_iota(jnp.int32, (blk, V7X_LANES), 1) < DIFF_D
    lam = (jnp.exp(jnp.sum(lq1_ref[...] * lk1_ref[...], axis=-1, keepdims=True))
           - jnp.exp(jnp.sum(lq2_ref[...] * lk2_ref[...], axis=-1, keepdims=True)) + lam_init)

    def key_of(ncols):
        return lax.broadcasted_iota(jnp.int32, (blk, ncols), 0)

    def qry_of(first_col, ncols):
        col = lax.broadcasted_iota(jnp.int32, (blk, ncols), 1) + first_col
        return col % blk + jnp.where(col >= bq, blk, 0)

    def keys(kj):
        return pl.ds(pl.multiple_of(kj * blk, blk), blk)

    def q_block(qi, carry):
        zero = jnp.zeros((blk, V7X_LANES), BF16)
        for half in range(2):
            q = q_ref[0, pl.ds(pl.multiple_of(qi * bq + half * blk, blk), blk), :]
            qq_ref[(2 * half) * blk:(2 * half + 1) * blk, :] = jnp.where(lo, q, zero)
            qq_ref[(2 * half + 1) * blk:(2 * half + 2) * blk, :] = jnp.where(lo, zero, q)
        _init_stats(m_ref, l_ref, acc_ref)

        def scores(kj):
            return _dot_nt(k_ref[0, keys(kj), :], qq_ref[...]) + slope * key_of(2 * bq).astype(F32)

        def shift_of(kj):
            return slope * ((kj - 2 * qi) * blk).astype(F32)

        def consume(t_ref, kj, masked):
            t = t_ref[...]
            if masked:
                t = jnp.where(key_of(2 * bq) + (kj - 2 * qi) * blk <= qry_of(0, 2 * bq), t, NEG_INF)
            _online_update_t(m_ref, l_ref, acc_ref, t, shift_of(kj), vt_ref[:, keys(kj)])

        def last_scores(t_ref):
            t_ref[:, bq:] = (_dot_nt(k_ref[0, keys(2 * qi + 1), :], qq_ref[bq:, :])
                             + slope * key_of(bq).astype(F32))

        def last_consume(t_ref):
            t = jnp.where(key_of(bq) + blk <= qry_of(bq, bq), t_ref[:, bq:], NEG_INF)
            _online_update_t(m_ref.at[:, bq:], l_ref, acc_ref.at[:, bq:], t, shift_of(2 * qi + 1),
                             vt_ref[:, keys(2 * qi + 1)])

        _causal_sweep(qi, scores, consume, t0_ref, t1_ref, (last_scores, last_consume))

        acc = acc_ref[...]
        on = acc[:DIFF_V] * (1.0 / acc[DIFF_V:DIFF_V + 1])
        for half in range(2):
            o = on[:, (2 * half) * blk:(2 * half + 1) * blk] - lam * on[:, (2 * half + 1) * blk:(2 * half + 2) * blk]
            o = o * lax.rsqrt(jnp.sum(o * o, axis=0, keepdims=True) * (1.0 / DIFF_V) + EPS)
            rows = pl.ds(pl.multiple_of(qi * bq + half * blk, blk), blk)
            o_ref[0, rows, :] = ((o * g_ref[...]) * (1.0 - lam_init)).T.astype(o_ref.dtype)
        return carry

    lax.fori_loop(0, q_ref.shape[1] // bq, q_block, 0)


def _diff_attn(sc, qd, kd, vdt, lq1, lk1, lq2, lk2, gout, l, *, blk):
    B, S, _ = qd.shape
    bq = 2 * blk
    est = 2 * (3 * S * 128 * 2 + VT_ROWS * S * 2) + 4 * bq * 128 * 4 + 6 * blk * 2 * bq * 4
    return pl.pallas_call(
        functools.partial(_diff_attn_body, blk=blk),
        grid=(B, DIFF_HEADS),
        in_specs=[pl.BlockSpec(memory_space=pltpu.SMEM),
                  pl.BlockSpec((1, S, 128), lambda b, h: (b, 0, h)),
                  pl.BlockSpec((1, S, 128), lambda b, h: (b, 0, h)),
                  pl.BlockSpec((VT_ROWS, S), lambda b, h: (h, b)),
                  _layer_vec(lq1, l), _layer_vec(lk1, l), _layer_vec(lq2, l), _layer_vec(lk2, l),
                  _layer_vec(gout, l)],
        out_specs=pl.BlockSpec((1, S, 128), lambda b, h: (b, 0, h)),
        out_shape=jax.ShapeDtypeStruct((B, S, DIFF_HEADS * DIFF_V), BF16),
        scratch_shapes=[pltpu.VMEM((1, 2 * bq), F32),
                        pltpu.VMEM((VT_ROWS, 2 * bq), F32), pltpu.VMEM((2 * bq, 128), BF16),
                        pltpu.VMEM((blk, 2 * bq), F32), pltpu.VMEM((blk, 2 * bq), F32)],
        compiler_params=pltpu.CompilerParams(
            dimension_semantics=("arbitrary", "arbitrary"), vmem_limit_bytes=_vmem_limit(est)),
        name="diff_attn",
    )(sc, qd, kd, vdt, lq1, lk1, lq2, lk2, gout)


def _dil_attn_body(sc_ref, q_ref, k_ref, v_ref, o_ref, qs_ref, ks_ref, vs_ref, op_ref, lse_ref, *, S):
    h = pl.program_id(1)
    slope = sc_ref[h]
    blk = DIL_BLOCK
    ii = lax.broadcasted_iota(jnp.int32, (blk, 2 * blk), 0)
    jj = lax.broadcasted_iota(jnp.int32, (blk, 2 * blk), 1)
    back = ii + blk - jj
    in_band = (back >= 0) & (back <= blk)
    ones = jnp.ones((blk, V7X_LANES), BF16)

    for p, (window, r) in enumerate(DIL_PATTERNS):
        assert window // r == blk
        L = S // r
        nb = L // blk
        Lp = L + blk
        bias = jnp.where(in_band, (-slope * float(r)) * back.astype(F32), NEG_INF)

        def src_rows(start, r=r):
            return pl.ds(start, blk, stride=r) if r > 1 else pl.ds(start, blk)

        def zero_pad(c, carry, Lp=Lp):
            pad = pl.ds(pl.multiple_of(c * Lp, blk), blk)
            ks_ref[pad, :] = jnp.zeros((blk, V7X_LANES), BF16)
            vs_ref[pad, :] = jnp.zeros((blk, 2 * V7X_LANES), BF16)
            return carry

        lax.fori_loop(0, r, zero_pad, 0)

        def gather(idx, carry, r=r, nb=nb, L=L, Lp=Lp, src_rows=src_rows):
            c = idx // nb
            n = idx % nb
            src = src_rows(c + n * (blk * r))
            dst = pl.ds(pl.multiple_of(c * Lp + (n + 1) * blk, blk), blk)
            qs_ref[pl.ds(pl.multiple_of(c * L + n * blk, blk), blk), :] = q_ref[0, src, :].astype(BF16)
            ks_ref[dst, :] = k_ref[0, src, :].astype(BF16)
            vs_ref[dst, :] = jnp.concatenate([v_ref[0, src, :].astype(BF16), ones], axis=1)
            return carry

        lax.fori_loop(0, r * nb, gather, 0)

        def blocks(it, carry, r=r, nb=nb, L=L, Lp=Lp, p=p, bias=bias, src_rows=src_rows):
            for u in range(DIL_UNROLL):
                idx = it * DIL_UNROLL + u
                c = idx // nb
                n = idx % nb
                q = qs_ref[pl.ds(pl.multiple_of(c * L + n * blk, blk), blk), :]
                keys = pl.ds(pl.multiple_of(c * Lp + n * blk, blk), 2 * blk)
                s = _dot_nt(q, ks_ref[keys, :]) + bias
                s = jnp.where((n >= 1) | (jj >= blk), s, NEG_INF)
                m = jnp.max(s, axis=-1, keepdims=True)
                e = jnp.exp2(s - m)
                ov = _dot(e.astype(BF16), vs_ref[keys, :])
                den = ov[:, V7X_LANES:]
                dst = src_rows(c + n * (blk * r))
                op_ref[p, dst, :] = ov[:, :V7X_LANES] * (1.0 / den)
                lse_ref[p, dst, :] = m + jnp.log2(den)
            return carry

        lax.fori_loop(0, (r * nb) // DIL_UNROLL, blocks, 0)

    chunk = 256

    def combine(ci, carry):
        rows = pl.ds(pl.multiple_of(ci * chunk, chunk), chunk)
        l0 = lse_ref[0, rows, :]
        l1 = lse_ref[1, rows, :]
        l2 = lse_ref[2, rows, :]
        mm = jnp.maximum(jnp.maximum(l0, l1), l2)
        w0 = jnp.exp2(l0 - mm)
        w1 = jnp.exp2(l1 - mm)
        w2 = jnp.exp2(l2 - mm)
        num = w0 * op_ref[0, rows, :] + w1 * op_ref[1, rows, :] + w2 * op_ref[2, rows, :]
        o_ref[0, rows, :] = (num * (1.0 / (w0 + w1 + w2))).astype(o_ref.dtype)
        return carry

    lax.fori_loop(0, S // chunk, combine, 0)


def _dil_attn(sc, ql, kl, vl):
    B, S, _ = ql.shape
    r_max = max(r for _, r in DIL_PATTERNS)
    assert S % (DIL_BLOCK * r_max) == 0 and (S // DIL_BLOCK) % DIL_UNROLL == 0
    assert len(DIL_PATTERNS) == 3
    pad_rows = S + DIL_BLOCK * r_max
    est = (2 * (3 * S * 128 * 4 + S * 128 * 2) + 6 * S * 128 * 4
           + S * 128 * 2 + pad_rows * 128 * 2 + pad_rows * 256 * 2)
    return pl.pallas_call(
        functools.partial(_dil_attn_body, S=S),
        grid=(B, DIL_HEADS),
        in_specs=[pl.BlockSpec(memory_space=pltpu.SMEM),
                  pl.BlockSpec((1, S, 128), lambda b, h: (b, 0, h)),
                  pl.BlockSpec((1, S, 128), lambda b, h: (b, 0, h)),
                  pl.BlockSpec((1, S, 128), lambda b, h: (b, 0, h))],
        out_specs=pl.BlockSpec((1, S, 128), lambda b, h: (b, 0, h)),
        out_shape=jax.ShapeDtypeStruct((B, S, DIL_HEADS * DIL_D), BF16),
        scratch_shapes=[pltpu.VMEM((S, 128), BF16), pltpu.VMEM((pad_rows, 128), BF16),
                        pltpu.VMEM((pad_rows, 256), BF16),
                        pltpu.VMEM((3, S, 128), F32), pltpu.VMEM((3, S, 128), F32)],
        compiler_params=pltpu.CompilerParams(
            dimension_semantics=("arbitrary", "arbitrary"), vmem_limit_bytes=_vmem_limit(est)),
        name="dil_attn",
    )(sc, ql, kl, vl)


def _merge_body(x_ref, gz_ref, gb_ref, oa_ref, ob_ref, oc_ref, wa_ref, wb_ref, wc_ref, wo_ref, o_ref):
    D = x_ref.shape[-1]

    def gate(b):
        return jax.nn.sigmoid(gz_ref[:, b * D:(b + 1) * D].astype(F32) + gb_ref[:, b * D:(b + 1) * D])

    merged = (gate(0) * _dot(oa_ref[...], wa_ref[...])
              + gate(1) * _dot(ob_ref[...], wb_ref[...])
              + gate(2) * _dot(oc_ref[...], wc_ref[...]))
    o_ref[...] = x_ref[...] + _dot(merged.astype(BF16), wo_ref[...])


def _merge(x, z, gate_bias, oa, ob, oc, wa, wb, wc, wo, l, *, tm):
    T, D = x.shape

    def resident(a):
        return pl.BlockSpec((None,) + a.shape[1:], lambda i: (l, 0, 0), pipeline_mode=pl.Buffered(1))

    def rows(width):
        return pl.BlockSpec((tm, width), lambda i: (i, 0))

    zsz = jnp.dtype(z.dtype).itemsize
    est = (2 * tm * D * 4 * 2 + 2 * tm * N_BRANCH * D * zsz + 2 * 3 * tm * HW * 2
           + (3 * HW * D + D * D) * 2 + N_BRANCH * D * 4 + 4 * tm * D * 4)
    return pl.pallas_call(
        _merge_body,
        grid=(T // tm,),
        in_specs=[rows(D), rows(N_BRANCH * D), resident(gate_bias), rows(HW), rows(HW), rows(HW),
                  resident(wa), resident(wb), resident(wc), resident(wo)],
        out_specs=rows(D),
        out_shape=jax.ShapeDtypeStruct((T, D), F32),
        compiler_params=pltpu.CompilerParams(
            dimension_semantics=("arbitrary",), vmem_limit_bytes=_vmem_limit(est)),
        name="merge",
    )(x, z, gate_bias, oa, ob, oc, wa, wb, wc, wo)


def _ffn_up_body(x_ref, g_ref, wg_ref, wu_ref, cwg_ref, cwu_ref, cbg_ref, cbu_ref, o_ref, h_ref, *, tm, tiles_per_seq):
    i = pl.program_id(0)

    @pl.when(pl.program_id(1) == 0)
    def _():
        @pl.when(i % tiles_per_seq == 0)
        def _():
            h_ref[0:HALO, :] = jnp.zeros((HALO, h_ref.shape[1]), BF16)

        @pl.when(i % tiles_per_seq != 0)
        def _():
            h_ref[0:HALO, :] = h_ref[tm:tm + HALO, :]

        x = x_ref[...]
        h_ref[HALO:HALO + tm, :] = (_rms(x, x.shape[-1]) * g_ref[...]).astype(BF16)

    h = h_ref[...]

    def conv(u, cw_ref, cb_ref):
        return (u[HALO:HALO + tm] * cw_ref[2:3, :] + u[HALO - 1:HALO - 1 + tm] * cw_ref[1:2, :]
                + u[HALO - 2:HALO - 2 + tm] * cw_ref[0:1, :] + cb_ref[...])

    gate = conv(_dot(h, wg_ref[...]), cwg_ref, cbg_ref)
    up = conv(_dot(h, wu_ref[...]), cwu_ref, cbu_ref)
    o_ref[...] = (gate * jax.nn.sigmoid(gate) * up).astype(o_ref.dtype)


def _ffn_up(x, g, w_up, conv_w, conv_b, l, *, S, tm, tn):
    T, K = x.shape
    F = w_up.shape[2] // 2
    nj = F // tn
    est = 2 * tm * K * 4 + (tm + HALO) * K * 2 + 2 * 2 * K * tn * 2 + 2 * tm * tn * 2 + 6 * (tm + HALO) * tn * 4
    return pl.pallas_call(
        functools.partial(_ffn_up_body, tm=tm, tiles_per_seq=S // tm),
        grid=(T // tm, nj),
        in_specs=[pl.BlockSpec((tm, K), lambda i, j: (i, 0)),
                  _layer_vec(g, l),
                  pl.BlockSpec((None, K, tn), lambda i, j: (l, 0, j)),
                  pl.BlockSpec((None, K, tn), lambda i, j: (l, 0, j + nj)),
                  pl.BlockSpec((None, CONV_W, tn), lambda i, j: (l, 0, j)),
                  pl.BlockSpec((None, CONV_W, tn), lambda i, j: (l, 0, j + nj)),
                  pl.BlockSpec((None, 1, tn), lambda i, j: (l, 0, j)),
                  pl.BlockSpec((None, 1, tn), lambda i, j: (l, 0, j + nj))],
        out_specs=pl.BlockSpec((tm, tn), lambda i, j: (i, j)),
        out_shape=jax.ShapeDtypeStruct((T, F), BF16),
        scratch_shapes=[pltpu.VMEM((tm + HALO, K), BF16)],
        compiler_params=pltpu.CompilerParams(
            dimension_semantics=("arbitrary", "arbitrary"), vmem_limit_bytes=_vmem_limit(est)),
        name="ffn_up",
    )(x, g, w_up, w_up, conv_w, conv_w, conv_b, conv_b)


def _ffn_down_body(x_ref, a_ref, w_ref, o_ref):
    o_ref[...] = x_ref[...] + _dot(a_ref[...], w_ref[...])


def _ffn_down(x, act, w, l, *, tm):
    T, D = x.shape
    F = act.shape[1]
    est = 2 * (2 * tm * D * 4 + tm * F * 2) + F * D * 2
    return pl.pallas_call(
        _ffn_down_body,
        grid=(T // tm,),
        in_specs=[pl.BlockSpec((tm, D), lambda i: (i, 0)),
                  pl.BlockSpec((tm, F), lambda i: (i, 0)),
                  pl.BlockSpec((None, F, D), lambda i: (l, 0, 0), pipeline_mode=pl.Buffered(1))],
        out_specs=pl.BlockSpec((tm, D), lambda i: (i, 0)),
        out_shape=jax.ShapeDtypeStruct((T, D), F32),
        compiler_params=pltpu.CompilerParams(
            dimension_semantics=("arbitrary",), vmem_limit_bytes=_vmem_limit(est)),
        name="ffn_down",
    )(x, act, w)


def _rope_lanes(t):
    half = MLA_ROPE // 2
    z = jnp.zeros(t.shape[:-1] + (half,), t.dtype)
    return jnp.concatenate([t[..., :half], z, t[..., half:], z], axis=-1)


def _z_columns(D):
    col = {"gz": 0}
    c = N_BRANCH * D
    for name in ("dq", "dk", "dv", "lq", "lk", "lv"):
        col[name] = c
        c += HW
    col["cq"] = c
    col["ckv"] = c + 512
    col["kr"] = c + 1024
    return col, c + 1024 + 128


def _w_in_layout_body(w_ref, o_ref, *, D):
    o_cq, o_kr = 0, 1024
    o_d = o_kr + MLA_ROPE
    o_gz = o_d + 6 * HW
    n_gz = N_BRANCH * D
    col, used = _z_columns(D)
    tk = w_ref.shape[0]
    o_ref[:, col["gz"]:col["gz"] + n_gz] = w_ref[:, o_gz:o_gz + n_gz].astype(BF16)
    o_ref[:, col["dq"]:col["dq"] + 6 * HW] = w_ref[:, o_d:o_gz].astype(BF16)
    o_ref[:, col["cq"]:col["cq"] + o_kr] = w_ref[:, o_cq:o_kr].astype(BF16)
    half = MLA_ROPE // 2
    z = jnp.zeros((tk, half), BF16)
    o_ref[:, col["kr"]:col["kr"] + 128] = jnp.concatenate(
        [w_ref[:, o_kr:o_kr + half].astype(BF16), z, w_ref[:, o_kr + half:o_d].astype(BF16), z], axis=1)
    o_ref[:, used:] = jnp.zeros((tk, o_ref.shape[1] - used), BF16)


def _w_in_layout(w, D, n_pad, *, tk):
    depth, K, N = w.shape
    return pl.pallas_call(
        functools.partial(_w_in_layout_body, D=D),
        grid=(depth, K // tk),
        in_specs=[pl.BlockSpec((None, tk, N), lambda l, i: (l, i, 0))],
        out_specs=pl.BlockSpec((None, tk, n_pad), lambda l, i: (l, i, 0)),
        out_shape=jax.ShapeDtypeStruct((depth, K, n_pad), BF16),
        compiler_params=pltpu.CompilerParams(
            dimension_semantics=("arbitrary", "arbitrary"),
            vmem_limit_bytes=_vmem_limit(2 * tk * (N * 4 + n_pad * 2) + tk * N * 4)),
        name="w_in_layout",
    )(w)


def _w_uq_layout(w):
    r = w.reshape(w.shape[:-1] + (MLA_HEADS, MLA_NOPE + MLA_ROPE))
    out = jnp.concatenate([r[..., :MLA_NOPE], _rope_lanes(r[..., MLA_NOPE:])], axis=-1)
    return out.reshape(w.shape[:-1] + (MLA_HEADS * 2 * V7X_LANES,)).astype(BF16)


def _rope_tables(S):
    half = MLA_ROPE // 2
    inv = ROPE_THETA ** (-jnp.arange(half, dtype=F32) / half)
    ang = jnp.arange(S, dtype=F32)[:, None] * inv[None, :]
    cos, sin = jnp.cos(ang), jnp.sin(ang)
    z = jnp.zeros_like(cos)
    return jnp.concatenate([cos, z, cos, z], axis=-1), jnp.concatenate([-sin, z, sin, z], axis=-1)


def kernel(x, attn_norm_g, w_in, mla_q_lora_norm_g, mla_kv_lora_norm_g, mla_w_uq, mla_w_uk, mla_w_uv, mla_q_gain, mla_k_gain, diff_q_gain, diff_k_gain, diff_lam_q1, diff_lam_k1, diff_lam_q2, diff_lam_k2, diff_out_norm_g, dil_q_gain, dil_k_gain, gate_bias, w_br_mla, w_br_diff, w_br_dil, w_out, ffn_norm_g, ffn_w_up, ffn_conv_w, ffn_conv_b, ffn_w_down):
    B, S, D = x.shape
    depth = w_in.shape[0]
    T = B * S
    col, n_used = _z_columns(D)
    n_z = -(-n_used // 1024) * 1024
    cos_t, sin_t = _rope_tables(S)
    slopes = LOG2E * 2.0 ** (-8.0 * jnp.arange(1, DIFF_HEADS + DIL_HEADS + 1, dtype=F32) / (DIFF_HEADS + DIL_HEADS))
    s_mla = LOG2E * (MLA_NOPE + MLA_ROPE) ** -0.5
    rows = lambda v: v.reshape(depth, 1, -1).astype(F32)

    w_in_p = _w_in_layout(w_in, D, n_z, tk=128)
    w_uq_p = _w_uq_layout(mla_w_uq)
    w_uk_p = mla_w_uk.astype(BF16)
    w_uvt_p = jnp.swapaxes(mla_w_uv, 1, 2).astype(BF16)
    w_bra, w_brb, w_brc, w_out_p = (w.astype(BF16) for w in (w_br_mla, w_br_diff, w_br_dil, w_out))
    w_up_p, w_down_p = ffn_w_up.astype(BF16), ffn_w_down.astype(BF16)
    vecs = {
        "gql": rows(mla_q_lora_norm_g), "gkvl": rows(mla_kv_lora_norm_g),
        "gqn": rows(mla_q_gain[:, :MLA_NOPE] * s_mla), "gqr": rows(_rope_lanes(mla_q_gain[:, MLA_NOPE:]) * s_mla),
        "gkn": rows(mla_k_gain[:, :MLA_NOPE]), "gkr": rows(_rope_lanes(mla_k_gain[:, MLA_NOPE:])),
        "gdq": rows(jnp.tile(diff_q_gain, (1, 2)) * (LOG2E * DIFF_D ** -0.5)), "gdk": rows(jnp.tile(diff_k_gain, (1, 2))),
        "glq": rows(dil_q_gain * (LOG2E * DIL_D ** -0.5)), "glk": rows(dil_k_gain),
    }
    g_attn, g_ffn, g_bias, c_b = rows(attn_norm_g), rows(ffn_norm_g), rows(gate_bias), rows(ffn_conv_b)
    lam_q1, lam_k1, lam_q2, lam_k2 = rows(diff_lam_q1), rows(diff_lam_k1), rows(diff_lam_q2), rows(diff_lam_k2)
    g_dout = diff_out_norm_g.reshape(depth, DIFF_V, 1).astype(F32)
    sc_dil = jnp.concatenate([slopes[1::2], jnp.zeros((2,), F32)])

    xf = x.reshape(T, D)
    b3 = lambda a: a.reshape(B, S, a.shape[-1])
    flat = lambda a: a.reshape(T, a.shape[-1])
    for l in range(depth):
        lam_init = 0.8 - 0.6 * math.exp(-0.3 * l)
        z = _norm_matmul(xf, g_attn, w_in_p, l, tm=1024, tn=1024, out_dtype=BF16)
        qm, km, vmt, qd, kd, vdt, ql, kl, vl = _prep(
            z, cos_t, sin_t, vecs, w_uq_p, w_uk_p, w_uvt_p, l, S=S, tm=256, col=col)
        o_mla = _mla_attn(b3(qm), b3(km), vmt, blk=512)
        sc_diff = jnp.concatenate([slopes[0::2], jnp.full((2,), lam_init, F32)])
        o_diff = _diff_attn(sc_diff, b3(qd), b3(kd), vdt, lam_q1, lam_k1, lam_q2, lam_k2, g_dout, l, blk=512)
        o_dil = _dil_attn(sc_dil, b3(ql), b3(kl), b3(vl))
        xf = _merge(xf, z, g_bias, flat(o_mla), flat(o_diff), flat(o_dil), w_bra, w_brb, w_brc, w_out_p, l, tm=256)
        act = _ffn_up(xf, g_ffn, w_up_p, ffn_conv_w, c_b, l, S=S, tm=1024, tn=512)
        xf = _ffn_down(xf, act, w_down_p, l, tm=256)
    return xf.reshape(B, S, D)
```

```python
import functools
import math

import jax
import jax.numpy as jnp
from jax import lax
from jax.experimental import pallas as pl
from jax.experimental.pallas import tpu as pltpu

F32 = jnp.float32
BF16 = jnp.bfloat16
EPS = 1e-6
NEG_INF = float("-inf")
LOG2E = math.log2(math.e)

V7X_LANES = 128
V7X_VMEM_BYTES = 64 * 1024 * 1024
BF16_ROWS_PER_VREG = 16

MLA_HEADS = 6
MLA_NOPE = 128
MLA_ROPE = 64
MLA_V = 128
ROPE_THETA = 10000.0
DIFF_HEADS = 6
DIFF_D = 64
DIFF_V = 128
DIL_HEADS = 6
DIL_D = 128
DIL_PATTERNS = ((128, 1), (512, 4), (2048, 16))
DIL_BLOCK = 128
DIL_UNROLL = 16
N_BRANCH = 3
CONV_W = 3
HALO = BF16_ROWS_PER_VREG
HW = 6 * 128
VT_ROWS = DIFF_V + BF16_ROWS_PER_VREG


def _vmem_limit(block_bytes):
    return int(min(V7X_VMEM_BYTES - (4 << 20), block_bytes + (16 << 20)))


def _dot(a, b):
    return jnp.dot(a, b, preferred_element_type=F32)


def _dot_nt(a, b):
    return lax.dot_general(a, b, (((1,), (1,)), ((), ())), preferred_element_type=F32)


def _rms(x, n):
    return x * lax.rsqrt(jnp.sum(x * x, axis=-1, keepdims=True) * (1.0 / n) + EPS)


def _layer_vec(a, l):
    return pl.BlockSpec((None,) + a.shape[1:], lambda *_: (l, 0, 0))


def _norm_matmul_body(x_ref, g_ref, w_ref, o_ref, h_ref):
    @pl.when(pl.program_id(1) == 0)
    def _():
        x = x_ref[...]
        h_ref[...] = (_rms(x, x.shape[-1]) * g_ref[...]).astype(BF16)

    o_ref[...] = _dot(h_ref[...], w_ref[...]).astype(o_ref.dtype)


def _norm_matmul(x, g, w, l, *, tm, tn, out_dtype):
    T, K = x.shape
    N = w.shape[2]
    osz = jnp.dtype(out_dtype).itemsize
    est = 2 * tm * K * 4 + tm * K * 2 + 2 * K * tn * 2 + 2 * tm * tn * osz
    return pl.pallas_call(
        _norm_matmul_body,
        grid=(T // tm, N // tn),
        in_specs=[pl.BlockSpec((tm, K), lambda i, j: (i, 0)),
                  _layer_vec(g, l),
                  pl.BlockSpec((None, K, tn), lambda i, j: (l, 0, j))],
        out_specs=pl.BlockSpec((tm, tn), lambda i, j: (i, j)),
        out_shape=jax.ShapeDtypeStruct((T, N), out_dtype),
        scratch_shapes=[pltpu.VMEM((tm, K), BF16)],
        compiler_params=pltpu.CompilerParams(
            dimension_semantics=("arbitrary", "arbitrary"), vmem_limit_bytes=_vmem_limit(est)),
        name="norm_matmul",
    )(x, g, w)


def _rope(xb, c, s):
    return xb * c + pltpu.roll(xb, 64, 1) * s


def _prep_body(dq_ref, dk_ref, dv_ref, lq_ref, lk_ref, lv_ref, cq_ref, ckv_ref, kr_ref,
               cos_ref, sin_ref, gql_ref, gkvl_ref, wuq_ref, wuk_ref, wuvt_ref,
               gqn_ref, gqr_ref, gkn_ref, gkr_ref, gdq_ref, gdk_ref, glq_ref, glk_ref,
               qm_ref, km_ref, vmt_ref, qd_ref, kd_ref, vdt_ref, ql_ref, kl_ref, vl_ref):
    c = cos_ref[...]
    s = sin_ref[...]
    tm = c.shape[0]
    f32 = lambda ref: ref[...].astype(F32)

    cqn = (_rms(f32(cq_ref), cq_ref.shape[-1]) * gql_ref[...]).astype(BF16)
    q = _dot(cqn, wuq_ref[...])
    for h in range(MLA_HEADS):
        a = 2 * V7X_LANES * h
        qm_ref[:, a:a + 128] = (_rms(q[:, a:a + 128], MLA_NOPE) * gqn_ref[...]).astype(BF16)
        qr = _rms(q[:, a + 128:a + 256], MLA_ROPE) * gqr_ref[...]
        qm_ref[:, a + 128:a + 256] = _rope(qr, c, s).astype(BF16)
    ckvn = (_rms(f32(ckv_ref), ckv_ref.shape[-1]) * gkvl_ref[...]).astype(BF16)
    kn = _dot(ckvn, wuk_ref[...])
    kr = _rope(_rms(f32(kr_ref), MLA_ROPE) * gkr_ref[...], c, s).astype(BF16)
    for h in range(MLA_HEADS):
        a = V7X_LANES * h
        km_ref[:, 2 * a:2 * a + 128] = (_rms(kn[:, a:a + 128], MLA_NOPE) * gkn_ref[...]).astype(BF16)
        km_ref[:, 2 * a + 128:2 * a + 256] = kr
    vmt_ref[...] = _dot_nt(wuvt_ref[...], ckvn).astype(BF16)

    lo = lax.broadcasted_iota(jnp.int32, (tm, V7X_LANES), 1) < DIFF_D

    def half_rms(xh):
        sq = xh * xh
        s_lo = jnp.sum(jnp.where(lo, sq, 0.0), axis=-1, keepdims=True)
        s_hi = jnp.sum(jnp.where(lo, 0.0, sq), axis=-1, keepdims=True)
        inv = jnp.where(lo, lax.rsqrt(s_lo * (1.0 / DIFF_D) + EPS), lax.rsqrt(s_hi * (1.0 / DIFF_D) + EPS))
        return xh * inv

    for h in range(DIFF_HEADS):
        a = V7X_LANES * h
        qd_ref[:, a:a + 128] = (half_rms(dq_ref[:, a:a + 128].astype(F32)) * gdq_ref[...]).astype(BF16)
        kd_ref[:, a:a + 128] = (half_rms(dk_ref[:, a:a + 128].astype(F32)) * gdk_ref[...]).astype(BF16)
    vdt = f32(dv_ref).T.astype(BF16)
    for h in range(DIFF_HEADS):
        vdt_ref[VT_ROWS * h:VT_ROWS * h + DIFF_V, :] = vdt[DIFF_V * h:DIFF_V * (h + 1)]
        vdt_ref[VT_ROWS * h + DIFF_V:VT_ROWS * (h + 1), :] = jnp.ones((VT_ROWS - DIFF_V, tm), BF16)

    for h in range(DIL_HEADS):
        a = V7X_LANES * h
        ql_ref[:, a:a + 128] = _rms(lq_ref[:, a:a + 128].astype(F32), DIL_D) * glq_ref[...]
        kl_ref[:, a:a + 128] = _rms(lk_ref[:, a:a + 128].astype(F32), DIL_D) * glk_ref[...]
    vl_ref[...] = f32(lv_ref)


def _prep(z, cos_t, sin_t, vecs, wuq, wuk, wuvt, l, *, S, tm, col):
    T = z.shape[0]
    nseq = S // tm

    def zspec(width, start):
        assert start % width == 0
        return pl.BlockSpec((tm, width), lambda i, blk=start // width: (i, blk))

    def wspec(a):
        return pl.BlockSpec((None,) + a.shape[1:], lambda i: (l, 0, 0))

    in_specs = [zspec(HW, col["dq"]), zspec(HW, col["dk"]), zspec(HW, col["dv"]),
                zspec(HW, col["lq"]), zspec(HW, col["lk"]), zspec(HW, col["lv"]),
                zspec(512, col["cq"]), zspec(512, col["ckv"]), zspec(128, col["kr"]),
                pl.BlockSpec((tm, 128), lambda i: (i % nseq, 0)),
                pl.BlockSpec((tm, 128), lambda i: (i % nseq, 0))]
    names = ["gql", "gkvl"]
    names2 = ["gqn", "gqr", "gkn", "gkr", "gdq", "gdk", "glq", "glk"]
    in_specs += [_layer_vec(vecs[n], l) for n in names] + [wspec(wuq), wspec(wuk), wspec(wuvt)]
    in_specs += [_layer_vec(vecs[n], l) for n in names2]

    def ospec(width):
        return pl.BlockSpec((tm, width), lambda i: (i, 0))

    tspec = pl.BlockSpec((HW, tm), lambda i: (0, i))
    out_shape = [jax.ShapeDtypeStruct((T, 2 * HW), BF16),
                 jax.ShapeDtypeStruct((T, 2 * HW), BF16),
                 jax.ShapeDtypeStruct((HW, T), BF16),
                 jax.ShapeDtypeStruct((T, HW), BF16),
                 jax.ShapeDtypeStruct((T, HW), BF16),
                 jax.ShapeDtypeStruct((DIFF_HEADS * VT_ROWS, T), BF16),
                 jax.ShapeDtypeStruct((T, HW), F32),
                 jax.ShapeDtypeStruct((T, HW), F32),
                 jax.ShapeDtypeStruct((T, HW), F32)]
    vdspec = pl.BlockSpec((DIFF_HEADS * VT_ROWS, tm), lambda i: (0, i))
    out_specs = [ospec(2 * HW), ospec(2 * HW), tspec, ospec(HW), ospec(HW), vdspec, ospec(HW), ospec(HW), ospec(HW)]
    est = 2 * tm * (6 * HW + 1024 + 128 + 256) * 4 + 2 * 2 * 512 * 1536 * 2 + 2 * tm * (12 * HW) * 4
    return pl.pallas_call(
        _prep_body,
        grid=(T // tm,),
        in_specs=in_specs,
        out_specs=out_specs,
        out_shape=out_shape,
        compiler_params=pltpu.CompilerParams(
            dimension_semantics=("arbitrary",), vmem_limit_bytes=_vmem_limit(est)),
        name="prep",
    )(z, z, z, z, z, z, z, z, z, cos_t, sin_t, vecs["gql"], vecs["gkvl"], wuq, wuk, wuvt,
      *[vecs[n] for n in names2])


def _online_update_t(m_ref, l_ref, acc_ref, t, shift, vt):
    m_prev = m_ref[...]
    m_new = jnp.maximum(m_prev, jnp.max(t, axis=0, keepdims=True) + shift)
    alpha = jnp.exp2(m_prev - m_new)
    p = jnp.exp2(t + (shift - m_new))
    if l_ref is not None:
        l_ref[...] = alpha * l_ref[...] + jnp.sum(p, axis=0, keepdims=True)
    acc_ref[...] = alpha * acc_ref[...] + _dot(vt, p.astype(BF16))
    m_ref[...] = m_new


def _init_stats(m_ref, l_ref, acc_ref):
    m_ref[...] = jnp.full(m_ref.shape, NEG_INF, F32)
    if l_ref is not None:
        l_ref[...] = jnp.zeros(l_ref.shape, F32)
    acc_ref[...] = jnp.zeros(acc_ref.shape, F32)


def _causal_sweep(qi, scores, consume, t0_ref, t1_ref, last):
    t0_ref[...] = scores(0)

    def pair(i, carry):
        t1_ref[...] = scores(2 * i + 1)
        consume(t0_ref, 2 * i, False)
        t0_ref[...] = scores(2 * i + 2)
        consume(t1_ref, 2 * i + 1, False)
        return carry

    lax.fori_loop(0, qi, pair, 0)
    last_scores, last_consume = last
    last_scores(t1_ref)
    consume(t0_ref, 2 * qi, True)
    last_consume(t1_ref)


def _mla_attn_body(q_ref, k_ref, vt_ref, o_ref, m_ref, l_ref, acc_ref, t0_ref, t1_ref, *, blk):
    bq = 2 * blk

    def keys(kj):
        return pl.ds(pl.multiple_of(kj * blk, blk), blk)

    def q_block(qi, carry):
        q_rows = pl.ds(pl.multiple_of(qi * bq, bq), bq)
        late_rows = pl.ds(pl.multiple_of(qi * bq + blk, blk), blk)
        _init_stats(m_ref, l_ref, acc_ref)

        def scores(kj):
            return _dot_nt(k_ref[0, keys(kj), :], q_ref[0, q_rows, :])

        def consume(t_ref, kj, masked):
            t = t_ref[...]
            if masked:
                key = lax.broadcasted_iota(jnp.int32, t.shape, 0) + (kj - 2 * qi) * blk
                qry = lax.broadcasted_iota(jnp.int32, t.shape, 1)
                t = jnp.where(key <= qry, t, NEG_INF)
            _online_update_t(m_ref, l_ref, acc_ref, t, 0.0, vt_ref[:, keys(kj)])

        def last_scores(t_ref):
            t_ref[:, blk:] = _dot_nt(k_ref[0, keys(2 * qi + 1), :], q_ref[0, late_rows, :])

        def last_consume(t_ref):
            t = t_ref[:, blk:]
            key = lax.broadcasted_iota(jnp.int32, t.shape, 0)
            qry = lax.broadcasted_iota(jnp.int32, t.shape, 1)
            t = jnp.where(key <= qry, t, NEG_INF)
            _online_update_t(m_ref.at[:, blk:], l_ref.at[:, blk:], acc_ref.at[:, blk:], t, 0.0,
                             vt_ref[:, keys(2 * qi + 1)])

        _causal_sweep(qi, scores, consume, t0_ref, t1_ref, (last_scores, last_consume))
        o_ref[0, q_rows, :] = (acc_ref[...] * (1.0 / l_ref[...])).T.astype(o_ref.dtype)
        return carry

    lax.fori_loop(0, q_ref.shape[1] // bq, q_block, 0)


def _mla_attn(qm, km, vmt, *, blk):
    B, S, _ = qm.shape
    bq = 2 * blk
    est = 2 * (2 * S * 256 * 2 + 2 * S * 128 * 2) + 3 * bq * 128 * 4 + 6 * blk * bq * 4
    return pl.pallas_call(
        functools.partial(_mla_attn_body, blk=blk),
        grid=(B, MLA_HEADS),
        in_specs=[pl.BlockSpec((1, S, 256), lambda b, h: (b, 0, h)),
                  pl.BlockSpec((1, S, 256), lambda b, h: (b, 0, h)),
                  pl.BlockSpec((128, S), lambda b, h: (h, b))],
        out_specs=pl.BlockSpec((1, S, 128), lambda b, h: (b, 0, h)),
        out_shape=jax.ShapeDtypeStruct((B, S, MLA_HEADS * MLA_V), BF16),
        scratch_shapes=[pltpu.VMEM((1, bq), F32), pltpu.VMEM((1, bq), F32), pltpu.VMEM((128, bq), F32),
                        pltpu.VMEM((blk, bq), F32), pltpu.VMEM((blk, bq), F32)],
        compiler_params=pltpu.CompilerParams(
            dimension_semantics=("arbitrary", "arbitrary"), vmem_limit_bytes=_vmem_limit(est)),
        name="mla_attn",
    )(qm, km, vmt)


def _diff_attn_body(sc_ref, q_ref, k_ref, vt_ref, lq1_ref, lk1_ref, lq2_ref, lk2_ref, g_ref, o_ref,
                    m_ref, acc_ref, qq_ref, t0_ref, t1_ref, *, blk):
    l_ref = None
    bq = 2 * blk
    h = pl.program_id(1)
    slope = sc_ref[h]
    lam_init = sc_ref[DIFF_HEADS]
    lo = lax.broadcasted_iota(jnp.int32, (blk, V7X_LANES), 1) < DIFF_D
    lam = (jnp.exp(jnp.sum(lq1_ref[...] * lk1_ref[...], axis=-1, keepdims=True))
           - jnp.exp(jnp.sum(lq2_ref[...] * lk2_ref[...], axis=-1, keepdims=True)) + lam_init)

    def key_of(ncols):
        return lax.broadcasted_iota(jnp.int32, (blk, ncols), 0)

    def qry_of(first_col, ncols):
        col = lax.broadcasted_iota(jnp.int32, (blk, ncols), 1) + first_col
        return col % blk + jnp.where(col >= bq, blk, 0)

    def keys(kj):
        return pl.ds(pl.multiple_of(kj * blk, blk), blk)

    def q_block(qi, carry):
        zero = jnp.zeros((blk, V7X_LANES), BF16)
        for half in range(2):
            q = q_ref[0, pl.ds(pl.multiple_of(qi * bq + half * blk, blk), blk), :]
            qq_ref[(2 * half) * blk:(2 * half + 1) * blk, :] = jnp.where(lo, q, zero)
            qq_ref[(2 * half + 1) * blk:(2 * half + 2) * blk, :] = jnp.where(lo, zero, q)
        _init_stats(m_ref, l_ref, acc_ref)

        def scores(kj):
            return _dot_nt(k_ref[0, keys(kj), :], qq_ref[...]) + slope * key_of(2 * bq).astype(F32)

        def shift_of(kj):
            return slope * jnp.asarray((kj - 2 * qi) * blk, F32)

        def consume(t_ref, kj, masked):
            t = t_ref[...]
            if masked:
                t = jnp.where(key_of(2 * bq) + (kj - 2 * qi) * blk <= qry_of(0, 2 * bq), t, NEG_INF)
            _online_update_t(m_ref, l_ref, acc_ref, t, shift_of(kj), vt_ref[:, keys(kj)])

        def last_scores(t_ref):
            t_ref[:, bq:] = (_dot_nt(k_ref[0, keys(2 * qi + 1), :], qq_ref[bq:, :])
                             + slope * key_of(bq).astype(F32))

        def last_consume(t_ref):
            t = jnp.where(key_of(bq) + blk <= qry_of(bq, bq), t_ref[:, bq:], NEG_INF)
            _online_update_t(m_ref.at[:, bq:], l_ref, acc_ref.at[:, bq:], t, shift_of(2 * qi + 1),
                             vt_ref[:, keys(2 * qi + 1)])

        _causal_sweep(qi, scores, consume, t0_ref, t1_ref, (last_scores, last_consume))

        acc = acc_ref[...]
        on = acc[:DIFF_V] * (1.0 / acc[DIFF_V:DIFF_V + 1])
        for half in range(2):
            o = on[:, (2 * half) * blk:(2 * half + 1) * blk] - lam * on[:, (2 * half + 1) * blk:(2 * half + 2) * blk]
            o = o * lax.rsqrt(jnp.sum(o * o, axis=0, keepdims=True) * (1.0 / DIFF_V) + EPS)
            rows = pl.ds(pl.multiple_of(qi * bq + half * blk, blk), blk)
            o_ref[0, rows, :] = ((o * g_ref[...]) * (1.0 - lam_init)).T.astype(o_ref.dtype)
        return carry

    lax.fori_loop(0, q_ref.shape[1] // bq, q_block, 0)


def _diff_attn(sc, qd, kd, vdt, lq1, lk1, lq2, lk2, gout, l, *, blk):
    B, S, _ = qd.shape
    bq = 2 * blk
    est = 2 * (3 * S * 128 * 2 + VT_ROWS * S * 2) + 4 * bq * 128 * 4 + 6 * blk * 2 * bq * 4
    return pl.pallas_call(
        functools.partial(_diff_attn_body, blk=blk),
        grid=(B, DIFF_HEADS),
        in_specs=[pl.BlockSpec(memory_space=pltpu.SMEM),
                  pl.BlockSpec((1, S, 128), lambda b, h: (b, 0, h)),
                  pl.BlockSpec((1, S, 128), lambda b, h: (b, 0, h)),
                  pl.BlockSpec((VT_ROWS, S), lambda b, h: (h, b)),
                  _layer_vec(lq1, l), _layer_vec(lk1, l), _layer_vec(lq2, l), _layer_vec(lk2, l),
                  _layer_vec(gout, l)],
        out_specs=pl.BlockSpec((1, S, 128), lambda b, h: (b, 0, h)),
        out_shape=jax.ShapeDtypeStruct((B, S, DIFF_HEADS * DIFF_V), BF16),
        scratch_shapes=[pltpu.VMEM((1, 2 * bq), F32),
                        pltpu.VMEM((VT_ROWS, 2 * bq), F32), pltpu.VMEM((2 * bq, 128), BF16),
                        pltpu.VMEM((blk, 2 * bq), F32), pltpu.VMEM((blk, 2 * bq), F32)],
        compiler_params=pltpu.CompilerParams(
            dimension_semantics=("arbitrary", "arbitrary"), vmem_limit_bytes=_vmem_limit(est)),
        name="diff_attn",
    )(sc, qd, kd, vdt, lq1, lk1, lq2, lk2, gout)


def _dil_attn_body(sc_ref, q_ref, k_ref, v_ref, o_ref, qs_ref, ks_ref, vs_ref, op_ref, lse_ref, fq_ref, fk_ref, fv_ref, *, S):
    h = pl.program_id(1)
    slope = sc_ref[h]
    blk = DIL_BLOCK
    ii = lax.broadcasted_iota(jnp.int32, (blk, 2 * blk), 0)
    jj = lax.broadcasted_iota(jnp.int32, (blk, 2 * blk), 1)
    back = ii + blk - jj
    in_band = (back >= 0) & (back <= blk)
    ones = jnp.ones((blk, V7X_LANES), BF16)

    for p, (window, r) in enumerate(DIL_PATTERNS):
        assert window // r == blk
        L = S // r
        nb = L // blk
        Lp = L + blk
        bias = jnp.where(in_band, (-slope * float(r)) * back.astype(F32), NEG_INF)

        def src_rows(start, r=r):
            return pl.ds(start, blk, stride=r) if r > 1 else pl.ds(start, blk)

        r_prev = DIL_PATTERNS[p - 1][1] if p > 0 else 1
        from_stage = r_prev > 1 and r % r_prev == 0
        to_stage = p + 1 < len(DIL_PATTERNS) and r > 1 and DIL_PATTERNS[p + 1][1] % r == 0

        def read(which, c, n, r=r, r_prev=r_prev, from_stage=from_stage, src_rows=src_rows):
            if from_stage:
                f = r // r_prev
                rows = pl.ds((c % r_prev) * (S // r_prev) + c // r_prev + n * (blk * f), blk, stride=f)
                return (fq_ref, fk_ref, fv_ref)[which][rows, :]
            return (q_ref, k_ref, v_ref)[which][0, src_rows(c + n * (blk * r)), :]

        def zero_pad(c, carry, Lp=Lp):
            pad = pl.ds(pl.multiple_of(c * Lp, blk), blk)
            ks_ref[pad, :] = jnp.zeros((blk, V7X_LANES), BF16)
            vs_ref[pad, :] = jnp.zeros((blk, 2 * V7X_LANES), BF16)
            return carry

        lax.fori_loop(0, r, zero_pad, 0)

        def gather(idx, carry, nb=nb, L=L, Lp=Lp, read=read, to_stage=to_stage):
            c = idx // nb
            n = idx % nb
            qv, kv, vv = read(0, c, n), read(1, c, n), read(2, c, n)
            flat = pl.ds(pl.multiple_of(c * L + n * blk, blk), blk)
            dst = pl.ds(pl.multiple_of(c * Lp + (n + 1) * blk, blk), blk)
            if to_stage:
                fq_ref[flat, :] = qv
                fk_ref[flat, :] = kv
                fv_ref[flat, :] = vv
            qs_ref[flat, :] = qv.astype(BF16)
            ks_ref[dst, :] = kv.astype(BF16)
            vs_ref[dst, :] = jnp.concatenate([vv.astype(BF16), ones], axis=1)
            return carry

        lax.fori_loop(0, r * nb, gather, 0)

        def blocks(it, carry, r=r, nb=nb, L=L, Lp=Lp, p=p, bias=bias, src_rows=src_rows):
            for u in range(DIL_UNROLL):
                idx = it * DIL_UNROLL + u
                c = idx // nb
                n = idx % nb
                q = qs_ref[pl.ds(pl.multiple_of(c * L + n * blk, blk), blk), :]
                keys = pl.ds(pl.multiple_of(c * Lp + n * blk, blk), 2 * blk)
                s = _dot_nt(q, ks_ref[keys, :]) + bias
                s = jnp.where((n >= 1) | (jj >= blk), s, NEG_INF)
                m = jnp.max(s, axis=-1, keepdims=True)
                e = jnp.exp2(s - m)
                ov = _dot(e.astype(BF16), vs_ref[keys, :])
                den = ov[:, V7X_LANES:]
                dst = src_rows(c + n * (blk * r))
                op_ref[p, dst, :] = ov[:, :V7X_LANES] * (1.0 / den)
                lse_ref[p, dst, :] = m + jnp.log2(den)
            return carry

        lax.fori_loop(0, (r * nb) // DIL_UNROLL, blocks, 0)

    chunk = 256

    def combine(ci, carry):
        rows = pl.ds(pl.multiple_of(ci * chunk, chunk), chunk)
        l0 = lse_ref[0, rows, :]
        l1 = lse_ref[1, rows, :]
        l2 = lse_ref[2, rows, :]
        mm = jnp.maximum(jnp.maximum(l0, l1), l2)
        w0 = jnp.exp2(l0 - mm)
        w1 = jnp.exp2(l1 - mm)
        w2 = jnp.exp2(l2 - mm)
        num = w0 * op_ref[0, rows, :] + w1 * op_ref[1, rows, :] + w2 * op_ref[2, rows, :]
        o_ref[0, rows, :] = (num * (1.0 / (w0 + w1 + w2))).astype(o_ref.dtype)
        return carry

    lax.fori_loop(0, S // chunk, combine, 0)


def _dil_attn(sc, ql, kl, vl):
    B, S, _ = ql.shape
    r_max = max(r for _, r in DIL_PATTERNS)
    assert S % (DIL_BLOCK * r_max) == 0 and (S // DIL_BLOCK) % DIL_UNROLL == 0
    assert len(DIL_PATTERNS) == 3
    pad_rows = S + DIL_BLOCK * r_max
    est = (2 * (3 * S * 128 * 4 + S * 128 * 2) + 6 * S * 128 * 4
           + S * 128 * 2 + pad_rows * 128 * 2 + pad_rows * 256 * 2 + 3 * S * 128 * 4)
    return pl.pallas_call(
        functools.partial(_dil_attn_body, S=S),
        grid=(B, DIL_HEADS),
        in_specs=[pl.BlockSpec(memory_space=pltpu.SMEM),
                  pl.BlockSpec((1, S, 128), lambda b, h: (b, 0, h)),
                  pl.BlockSpec((1, S, 128), lambda b, h: (b, 0, h)),
                  pl.BlockSpec((1, S, 128), lambda b, h: (b, 0, h))],
        out_specs=pl.BlockSpec((1, S, 128), lambda b, h: (b, 0, h)),
        out_shape=jax.ShapeDtypeStruct((B, S, DIL_HEADS * DIL_D), BF16),
        scratch_shapes=[pltpu.VMEM((S, 128), BF16), pltpu.VMEM((pad_rows, 128), BF16),
                        pltpu.VMEM((pad_rows, 256), BF16),
                        pltpu.VMEM((3, S, 128), F32), pltpu.VMEM((3, S, 128), F32),
                        pltpu.VMEM((S, 128), F32), pltpu.VMEM((S, 128), F32), pltpu.VMEM((S, 128), F32)],
        compiler_params=pltpu.CompilerParams(
            dimension_semantics=("arbitrary", "arbitrary"), vmem_limit_bytes=_vmem_limit(est)),
        name="dil_attn",
    )(sc, ql, kl, vl)


def _merge_body(x_ref, gz_ref, gb_ref, oa_ref, ob_ref, oc_ref, wa_ref, wb_ref, wc_ref, wo_ref, o_ref):
    D = x_ref.shape[-1]

    def gate(b):
        return jax.nn.sigmoid(gz_ref[:, b * D:(b + 1) * D].astype(F32) + gb_ref[:, b * D:(b + 1) * D])

    merged = (gate(0) * _dot(oa_ref[...], wa_ref[...])
              + gate(1) * _dot(ob_ref[...], wb_ref[...])
              + gate(2) * _dot(oc_ref[...], wc_ref[...]))
    o_ref[...] = x_ref[...] + _dot(merged.astype(BF16), wo_ref[...])


def _merge(x, z, gate_bias, oa, ob, oc, wa, wb, wc, wo, l, *, tm):
    T, D = x.shape

    def resident(a):
        return pl.BlockSpec((None,) + a.shape[1:], lambda i: (l, 0, 0), pipeline_mode=pl.Buffered(1))

    def rows(width):
        return pl.BlockSpec((tm, width), lambda i: (i, 0))

    zsz = jnp.dtype(z.dtype).itemsize
    est = (2 * tm * D * 4 * 2 + 2 * tm * N_BRANCH * D * zsz + 2 * 3 * tm * HW * 2
           + (3 * HW * D + D * D) * 2 + N_BRANCH * D * 4 + 4 * tm * D * 4)
    return pl.pallas_call(
        _merge_body,
        grid=(T // tm,),
        in_specs=[rows(D), rows(N_BRANCH * D), resident(gate_bias), rows(HW), rows(HW), rows(HW),
                  resident(wa), resident(wb), resident(wc), resident(wo)],
        out_specs=rows(D),
        out_shape=jax.ShapeDtypeStruct((T, D), F32),
        compiler_params=pltpu.CompilerParams(
            dimension_semantics=("arbitrary",), vmem_limit_bytes=_vmem_limit(est)),
        name="merge",
    )(x, z, gate_bias, oa, ob, oc, wa, wb, wc, wo)


def _ffn_up_body(x_ref, g_ref, wg_ref, wu_ref, cwg_ref, cwu_ref, cbg_ref, cbu_ref, o_ref, h_ref, *, tm, tiles_per_seq):
    i = pl.program_id(0)

    @pl.when(pl.program_id(1) == 0)
    def _():
        @pl.when(i % tiles_per_seq == 0)
        def _():
            h_ref[0:HALO, :] = jnp.zeros((HALO, h_ref.shape[1]), BF16)

        @pl.when(i % tiles_per_seq != 0)
        def _():
            h_ref[0:HALO, :] = h_ref[tm:tm + HALO, :]

        x = x_ref[...]
        h_ref[HALO:HALO + tm, :] = (_rms(x, x.shape[-1]) * g_ref[...]).astype(BF16)

    h = h_ref[...]

    def conv(u, cw_ref, cb_ref):
        return (u[HALO:HALO + tm] * cw_ref[2:3, :] + u[HALO - 1:HALO - 1 + tm] * cw_ref[1:2, :]
                + u[HALO - 2:HALO - 2 + tm] * cw_ref[0:1, :] + cb_ref[...])

    gate = conv(_dot(h, wg_ref[...]), cwg_ref, cbg_ref)
    up = conv(_dot(h, wu_ref[...]), cwu_ref, cbu_ref)
    o_ref[...] = (gate * jax.nn.sigmoid(gate) * up).astype(o_ref.dtype)


def _ffn_up(x, g, w_up, conv_w, conv_b, l, *, S, tm, tn):
    T, K = x.shape
    F = w_up.shape[2] // 2
    nj = F // tn
    est = 2 * tm * K * 4 + (tm + HALO) * K * 2 + 2 * 2 * K * tn * 2 + 2 * tm * tn * 2 + 6 * (tm + HALO) * tn * 4
    return pl.pallas_call(
        functools.partial(_ffn_up_body, tm=tm, tiles_per_seq=S // tm),
        grid=(T // tm, nj),
        in_specs=[pl.BlockSpec((tm, K), lambda i, j: (i, 0)),
                  _layer_vec(g, l),
                  pl.BlockSpec((None, K, tn), lambda i, j: (l, 0, j)),
                  pl.BlockSpec((None, K, tn), lambda i, j: (l, 0, j + nj)),
                  pl.BlockSpec((None, CONV_W, tn), lambda i, j: (l, 0, j)),
                  pl.BlockSpec((None, CONV_W, tn), lambda i, j: (l, 0, j + nj)),
                  pl.BlockSpec((None, 1, tn), lambda i, j: (l, 0, j)),
                  pl.BlockSpec((None, 1, tn), lambda i, j: (l, 0, j + nj))],
        out_specs=pl.BlockSpec((tm, tn), lambda i, j: (i, j)),
        out_shape=jax.ShapeDtypeStruct((T, F), BF16),
        scratch_shapes=[pltpu.VMEM((tm + HALO, K), BF16)],
        compiler_params=pltpu.CompilerParams(
            dimension_semantics=("arbitrary", "arbitrary"), vmem_limit_bytes=_vmem_limit(est)),
        name="ffn_up",
    )(x, g, w_up, w_up, conv_w, conv_w, conv_b, conv_b)


def _ffn_down_body(x_ref, a_ref, w_ref, o_ref):
    o_ref[...] = x_ref[...] + _dot(a_ref[...], w_ref[...])


def _ffn_down(x, act, w, l, *, tm):
    T, D = x.shape
    F = act.shape[1]
    est = 2 * (2 * tm * D * 4 + tm * F * 2) + F * D * 2
    return pl.pallas_call(
        _ffn_down_body,
        grid=(T // tm,),
        in_specs=[pl.BlockSpec((tm, D), lambda i: (i, 0)),
                  pl.BlockSpec((tm, F), lambda i: (i, 0)),
                  pl.BlockSpec((None, F, D), lambda i: (l, 0, 0), pipeline_mode=pl.Buffered(1))],
        out_specs=pl.BlockSpec((tm, D), lambda i: (i, 0)),
        out_shape=jax.ShapeDtypeStruct((T, D), F32),
        compiler_params=pltpu.CompilerParams(
            dimension_semantics=("arbitrary",), vmem_limit_bytes=_vmem_limit(est)),
        name="ffn_down",
    )(x, act, w)


def _rope_lanes(t):
    half = MLA_ROPE // 2
    z = jnp.zeros(t.shape[:-1] + (half,), t.dtype)
    return jnp.concatenate([t[..., :half], z, t[..., half:], z], axis=-1)


def _z_columns(D):
    col = {"gz": 0}
    c = N_BRANCH * D
    for name in ("dq", "dk", "dv", "lq", "lk", "lv"):
        col[name] = c
        c += HW
    col["cq"] = c
    col["ckv"] = c + 512
    col["kr"] = c + 1024
    return col, c + 1024 + 128


def _w_in_layout_body(w_ref, o_ref, *, D):
    o_cq, o_kr = 0, 1024
    o_d = o_kr + MLA_ROPE
    o_gz = o_d + 6 * HW
    n_gz = N_BRANCH * D
    col, used = _z_columns(D)
    tk = w_ref.shape[0]
    o_ref[:, col["gz"]:col["gz"] + n_gz] = w_ref[:, o_gz:o_gz + n_gz].astype(BF16)
    o_ref[:, col["dq"]:col["dq"] + 6 * HW] = w_ref[:, o_d:o_gz].astype(BF16)
    o_ref[:, col["cq"]:col["cq"] + o_kr] = w_ref[:, o_cq:o_kr].astype(BF16)
    half = MLA_ROPE // 2
    z = jnp.zeros((tk, half), BF16)
    o_ref[:, col["kr"]:col["kr"] + 128] = jnp.concatenate(
        [w_ref[:, o_kr:o_kr + half].astype(BF16), z, w_ref[:, o_kr + half:o_d].astype(BF16), z], axis=1)
    o_ref[:, used:] = jnp.zeros((tk, o_ref.shape[1] - used), BF16)


def _w_in_layout(w, D, n_pad, *, tk):
    depth, K, N = w.shape
    return pl.pallas_call(
        functools.partial(_w_in_layout_body, D=D),
        grid=(depth, K // tk),
        in_specs=[pl.BlockSpec((None, tk, N), lambda l, i: (l, i, 0))],
        out_specs=pl.BlockSpec((None, tk, n_pad), lambda l, i: (l, i, 0)),
        out_shape=jax.ShapeDtypeStruct((depth, K, n_pad), BF16),
        compiler_params=pltpu.CompilerParams(
            dimension_semantics=("arbitrary", "arbitrary"),
            vmem_limit_bytes=_vmem_limit(2 * tk * (N * 4 + n_pad * 2) + tk * N * 4)),
        name="w_in_layout",
    )(w)


def _w_uq_layout(w):
    r = w.reshape(w.shape[:-1] + (MLA_HEADS, MLA_NOPE + MLA_ROPE))
    out = jnp.concatenate([r[..., :MLA_NOPE], _rope_lanes(r[..., MLA_NOPE:])], axis=-1)
    return out.reshape(w.shape[:-1] + (MLA_HEADS * 2 * V7X_LANES,)).astype(BF16)


def _rope_tables(S):
    half = MLA_ROPE // 2
    inv = ROPE_THETA ** (-jnp.arange(half, dtype=F32) / half)
    ang = jnp.arange(S, dtype=F32)[:, None] * inv[None, :]
    cos, sin = jnp.cos(ang), jnp.sin(ang)
    z = jnp.zeros_like(cos)
    return jnp.concatenate([cos, z, cos, z], axis=-1), jnp.concatenate([-sin, z, sin, z], axis=-1)


def kernel(x, attn_norm_g, w_in, mla_q_lora_norm_g, mla_kv_lora_norm_g, mla_w_uq, mla_w_uk, mla_w_uv, mla_q_gain, mla_k_gain, diff_q_gain, diff_k_gain, diff_lam_q1, diff_lam_k1, diff_lam_q2, diff_lam_k2, diff_out_norm_g, dil_q_gain, dil_k_gain, gate_bias, w_br_mla, w_br_diff, w_br_dil, w_out, ffn_norm_g, ffn_w_up, ffn_conv_w, ffn_conv_b, ffn_w_down):
    B, S, D = x.shape
    depth = w_in.shape[0]
    T = B * S
    col, n_used = _z_columns(D)
    n_z = -(-n_used // 1024) * 1024
    cos_t, sin_t = _rope_tables(S)
    slopes = LOG2E * 2.0 ** (-8.0 * jnp.arange(1, DIFF_HEADS + DIL_HEADS + 1, dtype=F32) / (DIFF_HEADS + DIL_HEADS))
    s_mla = LOG2E * (MLA_NOPE + MLA_ROPE) ** -0.5
    rows = lambda v: v.reshape(depth, 1, -1).astype(F32)

    w_in_p = _w_in_layout(w_in, D, n_z, tk=128)
    w_uq_p = _w_uq_layout(mla_w_uq)
    w_uk_p = mla_w_uk.astype(BF16)
    w_uvt_p = jnp.swapaxes(mla_w_uv, 1, 2).astype(BF16)
    w_bra, w_brb, w_brc, w_out_p = (w.astype(BF16) for w in (w_br_mla, w_br_diff, w_br_dil, w_out))
    w_up_p, w_down_p = ffn_w_up.astype(BF16), ffn_w_down.astype(BF16)
    vecs = {
        "gql": rows(mla_q_lora_norm_g), "gkvl": rows(mla_kv_lora_norm_g),
        "gqn": rows(mla_q_gain[:, :MLA_NOPE] * s_mla), "gqr": rows(_rope_lanes(mla_q_gain[:, MLA_NOPE:]) * s_mla),
        "gkn": rows(mla_k_gain[:, :MLA_NOPE]), "gkr": rows(_rope_lanes(mla_k_gain[:, MLA_NOPE:])),
        "gdq": rows(jnp.tile(diff_q_gain, (1, 2)) * (LOG2E * DIFF_D ** -0.5)), "gdk": rows(jnp.tile(diff_k_gain, (1, 2))),
        "glq": rows(dil_q_gain * (LOG2E * DIL_D ** -0.5)), "glk": rows(dil_k_gain),
    }
    g_attn, g_ffn, g_bias, c_b = rows(attn_norm_g), rows(ffn_norm_g), rows(gate_bias), rows(ffn_conv_b)
    lam_q1, lam_k1, lam_q2, lam_k2 = rows(diff_lam_q1), rows(diff_lam_k1), rows(diff_lam_q2), rows(diff_lam_k2)
    g_dout = diff_out_norm_g.reshape(depth, DIFF_V, 1).astype(F32)
    sc_dil = jnp.concatenate([slopes[1::2], jnp.zeros((2,), F32)])

    xf = x.reshape(T, D)
    b3 = lambda a: a.reshape(B, S, a.shape[-1])
    flat = lambda a: a.reshape(T, a.shape[-1])
    for l in range(depth):
        lam_init = 0.8 - 0.6 * math.exp(-0.3 * l)
        z = _norm_matmul(xf, g_attn, w_in_p, l, tm=1024, tn=1024, out_dtype=BF16)
        qm, km, vmt, qd, kd, vdt, ql, kl, vl = _prep(
            z, cos_t, sin_t, vecs, w_uq_p, w_uk_p, w_uvt_p, l, S=S, tm=256, col=col)
        o_mla = _mla_attn(b3(qm), b3(km), vmt, blk=512)
        sc_diff = jnp.concatenate([slopes[0::2], jnp.full((2,), lam_init, F32)])
        o_diff = _diff_attn(sc_diff, b3(qd), b3(kd), vdt, lam_q1, lam_k1, lam_q2, lam_k2, g_dout, l, blk=512)
        o_dil = _dil_attn(sc_dil, b3(ql), b3(kl), b3(vl))
        xf = _merge(xf, z, g_bias, flat(o_mla), flat(o_diff), flat(o_dil), w_bra, w_brb, w_brc, w_out_p, l, tm=256)
        act = _ffn_up(xf, g_ffn, w_up_p, ffn_conv_w, c_b, l, S=S, tm=1024, tn=512)
        xf = _ffn_down(xf, act, w_down_p, l, tm=256)
    return xf.reshape(B, S, D)
```

```python
import functools
import math

import jax
import jax.numpy as jnp
from jax import lax
from jax.experimental import pallas as pl
from jax.experimental.pallas import tpu as pltpu

F32 = jnp.float32
BF16 = jnp.bfloat16
EPS = 1e-6
NEG_INF = float("-inf")
LOG2E = math.log2(math.e)

V7X_LANES = 128
V7X_VMEM_BYTES = 64 * 1024 * 1024
BF16_ROWS_PER_VREG = 16

MLA_HEADS = 6
MLA_NOPE = 128
MLA_ROPE = 64
MLA_V = 128
ROPE_THETA = 10000.0
DIFF_HEADS = 6
DIFF_D = 64
DIFF_V = 128
DIL_HEADS = 6
DIL_D = 128
DIL_PATTERNS = ((128, 1), (512, 4), (2048, 16))
DIL_BLOCK = 128
DIL_UNROLL = 16
N_BRANCH = 3
CONV_W = 3
HALO = BF16_ROWS_PER_VREG
HW = 6 * 128
VT_ROWS = DIFF_V + BF16_ROWS_PER_VREG


def _vmem_limit(block_bytes):
    return int(min(V7X_VMEM_BYTES - (4 << 20), block_bytes + (16 << 20)))


def _dot(a, b):
    return jnp.dot(a, b, preferred_element_type=F32)


def _dot_nt(a, b):
    return lax.dot_general(a, b, (((1,), (1,)), ((), ())), preferred_element_type=F32)


def _rms(x, n):
    return x * lax.rsqrt(jnp.sum(x * x, axis=-1, keepdims=True) * (1.0 / n) + EPS)


def _layer_vec(a, l):
    return pl.BlockSpec((None,) + a.shape[1:], lambda *_: (l, 0, 0))


def _norm_matmul_body(x_ref, g_ref, w_ref, o_ref, h_ref):
    @pl.when(pl.program_id(1) == 0)
    def _():
        x = x_ref[...]
        h_ref[...] = (_rms(x, x.shape[-1]) * g_ref[...]).astype(BF16)

    o_ref[...] = _dot(h_ref[...], w_ref[...]).astype(o_ref.dtype)


def _norm_matmul(x, g, w, l, *, tm, tn, out_dtype):
    T, K = x.shape
    N = w.shape[2]
    osz = jnp.dtype(out_dtype).itemsize
    est = 2 * tm * K * 4 + tm * K * 2 + 2 * K * tn * 2 + 2 * tm * tn * osz
    return pl.pallas_call(
        _norm_matmul_body,
        grid=(T // tm, N // tn),
        in_specs=[pl.BlockSpec((tm, K), lambda i, j: (i, 0)),
                  _layer_vec(g, l),
                  pl.BlockSpec((None, K, tn), lambda i, j: (l, 0, j))],
        out_specs=pl.BlockSpec((tm, tn), lambda i, j: (i, j)),
        out_shape=jax.ShapeDtypeStruct((T, N), out_dtype),
        scratch_shapes=[pltpu.VMEM((tm, K), BF16)],
        compiler_params=pltpu.CompilerParams(
            dimension_semantics=("arbitrary", "arbitrary"), vmem_limit_bytes=_vmem_limit(est)),
        name="norm_matmul",
    )(x, g, w)


def _rope(xb, c, s):
    return xb * c + pltpu.roll(xb, 64, 1) * s


def _prep_body(dq_ref, dk_ref, dv_ref, lq_ref, lk_ref, lv_ref, cq_ref, ckv_ref, kr_ref,
               cos_ref, sin_ref, gql_ref, gkvl_ref, wuq_ref, wuk_ref, wuvt_ref,
               gqn_ref, gqr_ref, gkn_ref, gkr_ref, gdq_ref, gdk_ref, glq_ref, glk_ref,
               qm_ref, km_ref, vmt_ref, qd_ref, kd_ref, vdt_ref, ql_ref, kl_ref, vl_ref):
    c = cos_ref[...]
    s = sin_ref[...]
    tm = c.shape[0]
    f32 = lambda ref: ref[...].astype(F32)

    cqn = (_rms(f32(cq_ref), cq_ref.shape[-1]) * gql_ref[...]).astype(BF16)
    q = _dot(cqn, wuq_ref[...])
    for h in range(MLA_HEADS):
        a = 2 * V7X_LANES * h
        qm_ref[:, a:a + 128] = (_rms(q[:, a:a + 128], MLA_NOPE) * gqn_ref[...]).astype(BF16)
        qr = _rms(q[:, a + 128:a + 256], MLA_ROPE) * gqr_ref[...]
        qm_ref[:, a + 128:a + 256] = _rope(qr, c, s).astype(BF16)
    ckvn = (_rms(f32(ckv_ref), ckv_ref.shape[-1]) * gkvl_ref[...]).astype(BF16)
    kn = _dot(ckvn, wuk_ref[...])
    kr = _rope(_rms(f32(kr_ref), MLA_ROPE) * gkr_ref[...], c, s).astype(BF16)
    for h in range(MLA_HEADS):
        a = V7X_LANES * h
        km_ref[:, 2 * a:2 * a + 128] = (_rms(kn[:, a:a + 128], MLA_NOPE) * gkn_ref[...]).astype(BF16)
        km_ref[:, 2 * a + 128:2 * a + 256] = kr
    vmt_ref[...] = _dot_nt(wuvt_ref[...], ckvn).astype(BF16)

    lo = lax.broadcasted_iota(jnp.int32, (tm, V7X_LANES), 1) < DIFF_D

    def half_rms(xh):
        sq = xh * xh
        s_lo = jnp.sum(jnp.where(lo, sq, 0.0), axis=-1, keepdims=True)
        s_hi = jnp.sum(jnp.where(lo, 0.0, sq), axis=-1, keepdims=True)
        inv = jnp.where(lo, lax.rsqrt(s_lo * (1.0 / DIFF_D) + EPS), lax.rsqrt(s_hi * (1.0 / DIFF_D) + EPS))
        return xh * inv

    for h in range(DIFF_HEADS):
        a = V7X_LANES * h
        qd_ref[:, a:a + 128] = (half_rms(dq_ref[:, a:a + 128].astype(F32)) * gdq_ref[...]).astype(BF16)
        kd_ref[:, a:a + 128] = (half_rms(dk_ref[:, a:a + 128].astype(F32)) * gdk_ref[...]).astype(BF16)
    vdt = f32(dv_ref).T.astype(BF16)
    for h in range(DIFF_HEADS):
        vdt_ref[VT_ROWS * h:VT_ROWS * h + DIFF_V, :] = vdt[DIFF_V * h:DIFF_V * (h + 1)]
        vdt_ref[VT_ROWS * h + DIFF_V:VT_ROWS * (h + 1), :] = jnp.ones((VT_ROWS - DIFF_V, tm), BF16)

    for h in range(DIL_HEADS):
        a = V7X_LANES * h
        ql_ref[:, a:a + 128] = _rms(lq_ref[:, a:a + 128].astype(F32), DIL_D) * glq_ref[...]
        kl_ref[:, a:a + 128] = _rms(lk_ref[:, a:a + 128].astype(F32), DIL_D) * glk_ref[...]
    vl_ref[...] = f32(lv_ref)


def _prep(z, cos_t, sin_t, vecs, wuq, wuk, wuvt, l, *, S, tm, col):
    T = z.shape[0]
    nseq = S // tm

    def zspec(width, start):
        assert start % width == 0
        return pl.BlockSpec((tm, width), lambda i, blk=start // width: (i, blk))

    def wspec(a):
        return pl.BlockSpec((None,) + a.shape[1:], lambda i: (l, 0, 0))

    in_specs = [zspec(HW, col["dq"]), zspec(HW, col["dk"]), zspec(HW, col["dv"]),
                zspec(HW, col["lq"]), zspec(HW, col["lk"]), zspec(HW, col["lv"]),
                zspec(512, col["cq"]), zspec(512, col["ckv"]), zspec(128, col["kr"]),
                pl.BlockSpec((tm, 128), lambda i: (i % nseq, 0)),
                pl.BlockSpec((tm, 128), lambda i: (i % nseq, 0))]
    names = ["gql", "gkvl"]
    names2 = ["gqn", "gqr", "gkn", "gkr", "gdq", "gdk", "glq", "glk"]
    in_specs += [_layer_vec(vecs[n], l) for n in names] + [wspec(wuq), wspec(wuk), wspec(wuvt)]
    in_specs += [_layer_vec(vecs[n], l) for n in names2]

    def ospec(width):
        return pl.BlockSpec((tm, width), lambda i: (i, 0))

    tspec = pl.BlockSpec((HW, tm), lambda i: (0, i))
    out_shape = [jax.ShapeDtypeStruct((T, 2 * HW), BF16),
                 jax.ShapeDtypeStruct((T, 2 * HW), BF16),
                 jax.ShapeDtypeStruct((HW, T), BF16),
                 jax.ShapeDtypeStruct((T, HW), BF16),
                 jax.ShapeDtypeStruct((T, HW), BF16),
                 jax.ShapeDtypeStruct((DIFF_HEADS * VT_ROWS, T), BF16),
                 jax.ShapeDtypeStruct((T, HW), F32),
                 jax.ShapeDtypeStruct((T, HW), F32),
                 jax.ShapeDtypeStruct((T, HW), F32)]
    vdspec = pl.BlockSpec((DIFF_HEADS * VT_ROWS, tm), lambda i: (0, i))
    out_specs = [ospec(2 * HW), ospec(2 * HW), tspec, ospec(HW), ospec(HW), vdspec, ospec(HW), ospec(HW), ospec(HW)]
    est = 2 * tm * (6 * HW + 1024 + 128 + 256) * 4 + 2 * 2 * 512 * 1536 * 2 + 2 * tm * (12 * HW) * 4
    return pl.pallas_call(
        _prep_body,
        grid=(T // tm,),
        in_specs=in_specs,
        out_specs=out_specs,
        out_shape=out_shape,
        compiler_params=pltpu.CompilerParams(
            dimension_semantics=("arbitrary",), vmem_limit_bytes=_vmem_limit(est)),
        name="prep",
    )(z, z, z, z, z, z, z, z, z, cos_t, sin_t, vecs["gql"], vecs["gkvl"], wuq, wuk, wuvt,
      *[vecs[n] for n in names2])


def _online_update_t(m_ref, l_ref, acc_ref, t, shift, vt):
    m_prev = m_ref[...]
    m_new = jnp.maximum(m_prev, jnp.max(t, axis=0, keepdims=True) + shift)
    alpha = jnp.exp2(m_prev - m_new)
    p = jnp.exp2(t + (shift - m_new))
    if l_ref is not None:
        l_ref[...] = alpha * l_ref[...] + jnp.sum(p, axis=0, keepdims=True)
    acc_ref[...] = alpha * acc_ref[...] + _dot(vt, p.astype(BF16))
    m_ref[...] = m_new


def _init_stats(m_ref, l_ref, acc_ref):
    m_ref[...] = jnp.full(m_ref.shape, NEG_INF, F32)
    if l_ref is not None:
        l_ref[...] = jnp.zeros(l_ref.shape, F32)
    acc_ref[...] = jnp.zeros(acc_ref.shape, F32)


def _causal_sweep(qi, scores, consume, t0_ref, t1_ref, last):
    t0_ref[...] = scores(0)

    def pair(i, carry):
        t1_ref[...] = scores(2 * i + 1)
        consume(t0_ref, 2 * i, False)
        t0_ref[...] = scores(2 * i + 2)
        consume(t1_ref, 2 * i + 1, False)
        return carry

    lax.fori_loop(0, qi, pair, 0)
    last_scores, last_consume = last
    last_scores(t1_ref)
    consume(t0_ref, 2 * qi, True)
    last_consume(t1_ref)


def _mla_attn_body(q_ref, k_ref, vt_ref, o_ref, m_ref, l_ref, acc_ref, t0_ref, t1_ref, *, blk):
    bq = 2 * blk

    def keys(kj):
        return pl.ds(pl.multiple_of(kj * blk, blk), blk)

    def q_block(qi, carry):
        q_rows = pl.ds(pl.multiple_of(qi * bq, bq), bq)
        late_rows = pl.ds(pl.multiple_of(qi * bq + blk, blk), blk)
        _init_stats(m_ref, l_ref, acc_ref)

        def scores(kj):
            return _dot_nt(k_ref[0, keys(kj), :], q_ref[0, q_rows, :])

        def consume(t_ref, kj, masked):
            t = t_ref[...]
            if masked:
                key = lax.broadcasted_iota(jnp.int32, t.shape, 0) + (kj - 2 * qi) * blk
                qry = lax.broadcasted_iota(jnp.int32, t.shape, 1)
                t = jnp.where(key <= qry, t, NEG_INF)
            _online_update_t(m_ref, l_ref, acc_ref, t, 0.0, vt_ref[:, keys(kj)])

        def last_scores(t_ref):
            t_ref[:, blk:] = _dot_nt(k_ref[0, keys(2 * qi + 1), :], q_ref[0, late_rows, :])

        def last_consume(t_ref):
            t = t_ref[:, blk:]
            key = lax.broadcasted_iota(jnp.int32, t.shape, 0)
            qry = lax.broadcasted_iota(jnp.int32, t.shape, 1)
            t = jnp.where(key <= qry, t, NEG_INF)
            _online_update_t(m_ref.at[:, blk:], l_ref.at[:, blk:], acc_ref.at[:, blk:], t, 0.0,
                             vt_ref[:, keys(2 * qi + 1)])

        _causal_sweep(qi, scores, consume, t0_ref, t1_ref, (last_scores, last_consume))
        o_ref[0, q_rows, :] = (acc_ref[...] * (1.0 / l_ref[...])).T.astype(o_ref.dtype)
        return carry

    lax.fori_loop(0, q_ref.shape[1] // bq, q_block, 0)


def _mla_attn(qm, km, vmt, *, blk):
    B, S, _ = qm.shape
    bq = 2 * blk
    est = 2 * (2 * S * 256 * 2 + 2 * S * 128 * 2) + 3 * bq * 128 * 4 + 6 * blk * bq * 4
    return pl.pallas_call(
        functools.partial(_mla_attn_body, blk=blk),
        grid=(B, MLA_HEADS),
        in_specs=[pl.BlockSpec((1, S, 256), lambda b, h: (b, 0, h)),
                  pl.BlockSpec((1, S, 256), lambda b, h: (b, 0, h)),
                  pl.BlockSpec((128, S), lambda b, h: (h, b))],
        out_specs=pl.BlockSpec((1, S, 128), lambda b, h: (b, 0, h)),
        out_shape=jax.ShapeDtypeStruct((B, S, MLA_HEADS * MLA_V), BF16),
        scratch_shapes=[pltpu.VMEM((1, bq), F32), pltpu.VMEM((1, bq), F32), pltpu.VMEM((128, bq), F32),
                        pltpu.VMEM((blk, bq), F32), pltpu.VMEM((blk, bq), F32)],
        compiler_params=pltpu.CompilerParams(
            dimension_semantics=("arbitrary", "arbitrary"), vmem_limit_bytes=_vmem_limit(est)),
        name="mla_attn",
    )(qm, km, vmt)


def _diff_attn_body(sc_ref, q_ref, k_ref, vt_ref, lq1_ref, lk1_ref, lq2_ref, lk2_ref, g_ref, o_ref,
                    m_ref, acc_ref, qq_ref, t0_ref, t1_ref, *, blk):
    l_ref = None
    bq = 2 * blk
    h = pl.program_id(1)
    slope = sc_ref[h]
    lam_init = sc_ref[DIFF_HEADS]
    lo = lax.broadcasted_iota(jnp.int32, (blk, V7X_LANES), 1) < DIFF_D
    lam = (jnp.exp(jnp.sum(lq1_ref[...] * lk1_ref[...], axis=-1, keepdims=True))
           - jnp.exp(jnp.sum(lq2_ref[...] * lk2_ref[...], axis=-1, keepdims=True)) + lam_init)

    def key_of(ncols):
        return lax.broadcasted_iota(jnp.int32, (blk, ncols), 0)

    def qry_of(first_col, ncols):
        col = lax.broadcasted_iota(jnp.int32, (blk, ncols), 1) + first_col
        return col % blk + jnp.where(col >= bq, blk, 0)

    def keys(kj):
        return pl.ds(pl.multiple_of(kj * blk, blk), blk)

    def q_block(qi, carry):
        zero = jnp.zeros((blk, V7X_LANES), BF16)
        for half in range(2):
            q = q_ref[0, pl.ds(pl.multiple_of(qi * bq + half * blk, blk), blk), :]
            qq_ref[(2 * half) * blk:(2 * half + 1) * blk, :] = jnp.where(lo, q, zero)
            qq_ref[(2 * half + 1) * blk:(2 * half + 2) * blk, :] = jnp.where(lo, zero, q)
        _init_stats(m_ref, l_ref, acc_ref)

        def scores(kj):
            return _dot_nt(k_ref[0, keys(kj), :], qq_ref[...]) + slope * key_of(2 * bq).astype(F32)

        def shift_of(kj):
            return slope * jnp.asarray((kj - 2 * qi) * blk, F32)

        def consume(t_ref, kj, masked):
            t = t_ref[...]
            if masked:
                t = jnp.where(key_of(2 * bq) + (kj - 2 * qi) * blk <= qry_of(0, 2 * bq), t, NEG_INF)
            _online_update_t(m_ref, l_ref, acc_ref, t, shift_of(kj), vt_ref[:, keys(kj)])

        def last_scores(t_ref):
            t_ref[:, bq:] = (_dot_nt(k_ref[0, keys(2 * qi + 1), :], qq_ref[bq:, :])
                             + slope * key_of(bq).astype(F32))

        def last_consume(t_ref):
            t = jnp.where(key_of(bq) + blk <= qry_of(bq, bq), t_ref[:, bq:], NEG_INF)
            _online_update_t(m_ref.at[:, bq:], l_ref, acc_ref.at[:, bq:], t, shift_of(2 * qi + 1),
                             vt_ref[:, keys(2 * qi + 1)])

        _causal_sweep(qi, scores, consume, t0_ref, t1_ref, (last_scores, last_consume))

        acc = acc_ref[...]
        on = acc[:DIFF_V] * (1.0 / acc[DIFF_V:DIFF_V + 1])
        for half in range(2):
            o = on[:, (2 * half) * blk:(2 * half + 1) * blk] - lam * on[:, (2 * half + 1) * blk:(2 * half + 2) * blk]
            o = o * lax.rsqrt(jnp.sum(o * o, axis=0, keepdims=True) * (1.0 / DIFF_V) + EPS)
            rows = pl.ds(pl.multiple_of(qi * bq + half * blk, blk), blk)
            o_ref[0, rows, :] = ((o * g_ref[...]) * (1.0 - lam_init)).T.astype(o_ref.dtype)
        return carry

    lax.fori_loop(0, q_ref.shape[1] // bq, q_block, 0)


def _diff_attn(sc, qd, kd, vdt, lq1, lk1, lq2, lk2, gout, l, *, blk):
    B, S, _ = qd.shape
    bq = 2 * blk
    est = 2 * (3 * S * 128 * 2 + VT_ROWS * S * 2) + 4 * bq * 128 * 4 + 6 * blk * 2 * bq * 4
    return pl.pallas_call(
        functools.partial(_diff_attn_body, blk=blk),
        grid=(B, DIFF_HEADS),
        in_specs=[pl.BlockSpec(memory_space=pltpu.SMEM),
                  pl.BlockSpec((1, S, 128), lambda b, h: (b, 0, h)),
                  pl.BlockSpec((1, S, 128), lambda b, h: (b, 0, h)),
                  pl.BlockSpec((VT_ROWS, S), lambda b, h: (h, b)),
                  _layer_vec(lq1, l), _layer_vec(lk1, l), _layer_vec(lq2, l), _layer_vec(lk2, l),
                  _layer_vec(gout, l)],
        out_specs=pl.BlockSpec((1, S, 128), lambda b, h: (b, 0, h)),
        out_shape=jax.ShapeDtypeStruct((B, S, DIFF_HEADS * DIFF_V), BF16),
        scratch_shapes=[pltpu.VMEM((1, 2 * bq), F32),
                        pltpu.VMEM((VT_ROWS, 2 * bq), F32), pltpu.VMEM((2 * bq, 128), BF16),
                        pltpu.VMEM((blk, 2 * bq), F32), pltpu.VMEM((blk, 2 * bq), F32)],
        compiler_params=pltpu.CompilerParams(
            dimension_semantics=("arbitrary", "arbitrary"), vmem_limit_bytes=_vmem_limit(est)),
        name="diff_attn",
    )(sc, qd, kd, vdt, lq1, lk1, lq2, lk2, gout)


def _dil_attn_body(sc_ref, q_ref, k_ref, v_ref, o_ref, qs_ref, ks_ref, vs_ref, op_ref, lse_ref, fq_ref, fk_ref, fv_ref, *, S):
    h = pl.program_id(1)
    slope = sc_ref[h]
    blk = DIL_BLOCK
    ii = lax.broadcasted_iota(jnp.int32, (blk, 2 * blk), 0)
    jj = lax.broadcasted_iota(jnp.int32, (blk, 2 * blk), 1)
    back = ii + blk - jj
    in_band = (back >= 0) & (back <= blk)
    ones = jnp.ones((blk, V7X_LANES), BF16)

    for p, (window, r) in enumerate(DIL_PATTERNS):
        assert window // r == blk
        L = S // r
        nb = L // blk
        Lp = L + blk
        bias = jnp.where(in_band, (-slope * float(r)) * back.astype(F32), NEG_INF)

        def src_rows(start, r=r):
            return pl.ds(start, blk, stride=r) if r > 1 else pl.ds(start, blk)

        r_prev = DIL_PATTERNS[p - 1][1] if p > 0 else 1
        from_stage = r_prev > 1 and r % r_prev == 0
        to_stage = p + 1 < len(DIL_PATTERNS) and r > 1 and DIL_PATTERNS[p + 1][1] % r == 0

        def read(which, c, n, r=r, r_prev=r_prev, from_stage=from_stage, src_rows=src_rows):
            if from_stage:
                f = r // r_prev
                rows = pl.ds((c % r_prev) * (S // r_prev) + c // r_prev + n * (blk * f), blk, stride=f)
                return (fq_ref, fk_ref, fv_ref)[which][rows, :]
            return (q_ref, k_ref, v_ref)[which][0, src_rows(c + n * (blk * r)), :]

        def zero_pad(c, carry, Lp=Lp):
            pad = pl.ds(pl.multiple_of(c * Lp, blk), blk)
            ks_ref[pad, :] = jnp.zeros((blk, V7X_LANES), BF16)
            vs_ref[pad, :] = jnp.zeros((blk, 2 * V7X_LANES), BF16)
            return carry

        lax.fori_loop(0, r, zero_pad, 0)

        def gather(idx, carry, nb=nb, L=L, Lp=Lp, read=read, to_stage=to_stage):
            c = idx // nb
            n = idx % nb
            qv, kv, vv = read(0, c, n), read(1, c, n), read(2, c, n)
            flat = pl.ds(pl.multiple_of(c * L + n * blk, blk), blk)
            dst = pl.ds(pl.multiple_of(c * Lp + (n + 1) * blk, blk), blk)
            if to_stage:
                fq_ref[flat, :] = qv
                fk_ref[flat, :] = kv
                fv_ref[flat, :] = vv
            qs_ref[flat, :] = qv.astype(BF16)
            ks_ref[dst, :] = kv.astype(BF16)
            vs_ref[dst, :] = jnp.concatenate([vv.astype(BF16), ones], axis=1)
            return carry

        lax.fori_loop(0, r * nb, gather, 0)

        def blocks(it, carry, r=r, nb=nb, L=L, Lp=Lp, p=p, bias=bias, src_rows=src_rows):
            for u in range(DIL_UNROLL):
                idx = it * DIL_UNROLL + u
                c = idx // nb
                n = idx % nb
                q = qs_ref[pl.ds(pl.multiple_of(c * L + n * blk, blk), blk), :]
                keys = pl.ds(pl.multiple_of(c * Lp + n * blk, blk), 2 * blk)
                s = _dot_nt(q, ks_ref[keys, :]) + bias
                s = jnp.where((n >= 1) | (jj >= blk), s, NEG_INF)
                m = jnp.max(s, axis=-1, keepdims=True)
                e = jnp.exp2(s - m)
                ov = _dot(e.astype(BF16), vs_ref[keys, :])
                den = ov[:, V7X_LANES:]
                dst = src_rows(c + n * (blk * r))
                op_ref[p, dst, :] = ov[:, :V7X_LANES] * (1.0 / den)
                lse_ref[p, dst, :] = m + jnp.log2(den)
            return carry

        lax.fori_loop(0, (r * nb) // DIL_UNROLL, blocks, 0)

    chunk = 256

    def combine(ci, carry):
        rows = pl.ds(pl.multiple_of(ci * chunk, chunk), chunk)
        l0 = lse_ref[0, rows, :]
        l1 = lse_ref[1, rows, :]
        l2 = lse_ref[2, rows, :]
        mm = jnp.maximum(jnp.maximum(l0, l1), l2)
        w0 = jnp.exp2(l0 - mm)
        w1 = jnp.exp2(l1 - mm)
        w2 = jnp.exp2(l2 - mm)
        num = w0 * op_ref[0, rows, :] + w1 * op_ref[1, rows, :] + w2 * op_ref[2, rows, :]
        o_ref[0, rows, :] = (num * (1.0 / (w0 + w1 + w2))).astype(o_ref.dtype)
        return carry

    lax.fori_loop(0, S // chunk, combine, 0)


def _dil_attn(sc, ql, kl, vl):
    B, S, _ = ql.shape
    r_max = max(r for _, r in DIL_PATTERNS)
    assert S % (DIL_BLOCK * r_max) == 0 and (S // DIL_BLOCK) % DIL_UNROLL == 0
    assert len(DIL_PATTERNS) == 3
    pad_rows = S + DIL_BLOCK * r_max
    est = (2 * (3 * S * 128 * 4 + S * 128 * 2) + 6 * S * 128 * 4
           + S * 128 * 2 + pad_rows * 128 * 2 + pad_rows * 256 * 2 + 3 * S * 128 * 4)
    return pl.pallas_call(
        functools.partial(_dil_attn_body, S=S),
        grid=(B, DIL_HEADS),
        in_specs=[pl.BlockSpec(memory_space=pltpu.SMEM),
                  pl.BlockSpec((1, S, 128), lambda b, h: (b, 0, h)),
                  pl.BlockSpec((1, S, 128), lambda b, h: (b, 0, h)),
                  pl.BlockSpec((1, S, 128), lambda b, h: (b, 0, h))],
        out_specs=pl.BlockSpec((1, S, 128), lambda b, h: (b, 0, h)),
        out_shape=jax.ShapeDtypeStruct((B, S, DIL_HEADS * DIL_D), BF16),
        scratch_shapes=[pltpu.VMEM((S, 128), BF16), pltpu.VMEM((pad_rows, 128), BF16),
                        pltpu.VMEM((pad_rows, 256), BF16),
                        pltpu.VMEM((3, S, 128), F32), pltpu.VMEM((3, S, 128), F32),
                        pltpu.VMEM((S, 128), F32), pltpu.VMEM((S, 128), F32), pltpu.VMEM((S, 128), F32)],
        compiler_params=pltpu.CompilerParams(
            dimension_semantics=("arbitrary", "arbitrary"), vmem_limit_bytes=_vmem_limit(est)),
        name="dil_attn",
    )(sc, ql, kl, vl)


def _merge_body(x_ref, gz_ref, gb_ref, oa_ref, ob_ref, oc_ref, wa_ref, wb_ref, wc_ref, wo_ref, o_ref):
    D = x_ref.shape[-1]

    def gate(b):
        return jax.nn.sigmoid(gz_ref[:, b * D:(b + 1) * D].astype(F32) + gb_ref[:, b * D:(b + 1) * D])

    merged = (gate(0) * _dot(oa_ref[...], wa_ref[...])
              + gate(1) * _dot(ob_ref[...], wb_ref[...])
              + gate(2) * _dot(oc_ref[...], wc_ref[...]))
    o_ref[...] = x_ref[...] + _dot(merged.astype(BF16), wo_ref[...])


def _merge(x, z, gate_bias, oa, ob, oc, wa, wb, wc, wo, l, *, tm):
    T, D = x.shape

    def resident(a):
        return pl.BlockSpec((None,) + a.shape[1:], lambda i: (l, 0, 0), pipeline_mode=pl.Buffered(1))

    def rows(width):
        return pl.BlockSpec((tm, width), lambda i: (i, 0))

    zsz = jnp.dtype(z.dtype).itemsize
    est = (2 * tm * D * 4 * 2 + 2 * tm * N_BRANCH * D * zsz + 2 * 3 * tm * HW * 2
           + (3 * HW * D + D * D) * 2 + N_BRANCH * D * 4 + 4 * tm * D * 4)
    return pl.pallas_call(
        _merge_body,
        grid=(T // tm,),
        in_specs=[rows(D), rows(N_BRANCH * D), resident(gate_bias), rows(HW), rows(HW), rows(HW),
                  resident(wa), resident(wb), resident(wc), resident(wo)],
        out_specs=rows(D),
        out_shape=jax.ShapeDtypeStruct((T, D), F32),
        compiler_params=pltpu.CompilerParams(
            dimension_semantics=("arbitrary",), vmem_limit_bytes=_vmem_limit(est)),
        name="merge",
    )(x, z, gate_bias, oa, ob, oc, wa, wb, wc, wo)


def _ffn_up_body(x_ref, g_ref, wg_ref, wu_ref, cwg_ref, cwu_ref, cbg_ref, cbu_ref, o_ref, h_ref, *, tm, tiles_per_seq):
    i = pl.program_id(0)

    @pl.when(pl.program_id(1) == 0)
    def _():
        @pl.when(i % tiles_per_seq == 0)
        def _():
            h_ref[0:HALO, :] = jnp.zeros((HALO, h_ref.shape[1]), BF16)

        @pl.when(i % tiles_per_seq != 0)
        def _():
            h_ref[0:HALO, :] = h_ref[tm:tm + HALO, :]

        x = x_ref[...]
        h_ref[HALO:HALO + tm, :] = (_rms(x, x.shape[-1]) * g_ref[...]).astype(BF16)

    h = h_ref[...]

    def conv(u, cw_ref, cb_ref):
        return (u[HALO:HALO + tm] * cw_ref[2:3, :] + u[HALO - 1:HALO - 1 + tm] * cw_ref[1:2, :]
                + u[HALO - 2:HALO - 2 + tm] * cw_ref[0:1, :] + cb_ref[...])

    gate = conv(_dot(h, wg_ref[...]), cwg_ref, cbg_ref)
    up = conv(_dot(h, wu_ref[...]), cwu_ref, cbu_ref)
    o_ref[...] = (gate * jax.nn.sigmoid(gate) * up).astype(o_ref.dtype)


def _ffn_up(x, g, w_up, conv_w, conv_b, l, *, S, tm, tn):
    T, K = x.shape
    F = w_up.shape[2] // 2
    nj = F // tn
    est = 2 * tm * K * 4 + (tm + HALO) * K * 2 + 2 * 2 * K * tn * 2 + 2 * tm * tn * 2 + 6 * (tm + HALO) * tn * 4
    return pl.pallas_call(
        functools.partial(_ffn_up_body, tm=tm, tiles_per_seq=S // tm),
        grid=(T // tm, nj),
        in_specs=[pl.BlockSpec((tm, K), lambda i, j: (i, 0)),
                  _layer_vec(g, l),
                  pl.BlockSpec((None, K, tn), lambda i, j: (l, 0, j)),
                  pl.BlockSpec((None, K, tn), lambda i, j: (l, 0, j + nj)),
                  pl.BlockSpec((None, CONV_W, tn), lambda i, j: (l, 0, j)),
                  pl.BlockSpec((None, CONV_W, tn), lambda i, j: (l, 0, j + nj)),
                  pl.BlockSpec((None, 1, tn), lambda i, j: (l, 0, j)),
                  pl.BlockSpec((None, 1, tn), lambda i, j: (l, 0, j + nj))],
        out_specs=pl.BlockSpec((tm, tn), lambda i, j: (i, j)),
        out_shape=jax.ShapeDtypeStruct((T, F), BF16),
        scratch_shapes=[pltpu.VMEM((tm + HALO, K), BF16)],
        compiler_params=pltpu.CompilerParams(
            dimension_semantics=("arbitrary", "arbitrary"), vmem_limit_bytes=_vmem_limit(est)),
        name="ffn_up",
    )(x, g, w_up, w_up, conv_w, conv_w, conv_b, conv_b)


def _ffn_down_body(x_ref, a_ref, w_ref, o_ref):
    o_ref[...] = x_ref[...] + _dot(a_ref[...], w_ref[...])


def _ffn_down(x, act, w, l, *, tm):
    T, D = x.shape
    F = act.shape[1]
    est = 2 * (2 * tm * D * 4 + tm * F * 2) + F * D * 2
    return pl.pallas_call(
        _ffn_down_body,
        grid=(T // tm,),
        in_specs=[pl.BlockSpec((tm, D), lambda i: (i, 0)),
                  pl.BlockSpec((tm, F), lambda i: (i, 0)),
                  pl.BlockSpec((None, F, D), lambda i: (l, 0, 0), pipeline_mode=pl.Buffered(1))],
        out_specs=pl.BlockSpec((tm, D), lambda i: (i, 0)),
        out_shape=jax.ShapeDtypeStruct((T, D), F32),
        compiler_params=pltpu.CompilerParams(
            dimension_semantics=("arbitrary",), vmem_limit_bytes=_vmem_limit(est)),
        name="ffn_down",
    )(x, act, w)


def _rope_lanes(t):
    half = MLA_ROPE // 2
    z = jnp.zeros(t.shape[:-1] + (half,), t.dtype)
    return jnp.concatenate([t[..., :half], z, t[..., half:], z], axis=-1)


def _z_columns(D):
    col = {"gz": 0}
    c = N_BRANCH * D
    for name in ("dq", "dk", "dv", "lq", "lk", "lv"):
        col[name] = c
        c += HW
    col["cq"] = c
    col["ckv"] = c + 512
    col["kr"] = c + 1024
    return col, c + 1024 + 128


def _w_in_layout_body(w_ref, o_ref, *, D):
    o_cq, o_kr = 0, 1024
    o_d = o_kr + MLA_ROPE
    o_gz = o_d + 6 * HW
    n_gz = N_BRANCH * D
    col, used = _z_columns(D)
    tk = w_ref.shape[0]
    o_ref[:, col["gz"]:col["gz"] + n_gz] = w_ref[:, o_gz:o_gz + n_gz].astype(BF16)
    o_ref[:, col["dq"]:col["dq"] + 6 * HW] = w_ref[:, o_d:o_gz].astype(BF16)
    o_ref[:, col["cq"]:col["cq"] + o_kr] = w_ref[:, o_cq:o_kr].astype(BF16)
    half = MLA_ROPE // 2
    z = jnp.zeros((tk, half), BF16)
    o_ref[:, col["kr"]:col["kr"] + 128] = jnp.concatenate(
        [w_ref[:, o_kr:o_kr + half].astype(BF16), z, w_ref[:, o_kr + half:o_d].astype(BF16), z], axis=1)
    o_ref[:, used:] = jnp.zeros((tk, o_ref.shape[1] - used), BF16)


def _w_in_layout(w, D, n_pad, *, tk):
    depth, K, N = w.shape
    return pl.pallas_call(
        functools.partial(_w_in_layout_body, D=D),
        grid=(depth, K // tk),
        in_specs=[pl.BlockSpec((None, tk, N), lambda l, i: (l, i, 0))],
        out_specs=pl.BlockSpec((None, tk, n_pad), lambda l, i: (l, i, 0)),
        out_shape=jax.ShapeDtypeStruct((depth, K, n_pad), BF16),
        compiler_params=pltpu.CompilerParams(
            dimension_semantics=("arbitrary", "arbitrary"),
            vmem_limit_bytes=_vmem_limit(2 * tk * (N * 4 + n_pad * 2) + tk * N * 4)),
        name="w_in_layout",
    )(w)


def _w_uq_layout(w):
    r = w.reshape(w.shape[:-1] + (MLA_HEADS, MLA_NOPE + MLA_ROPE))
    out = jnp.concatenate([r[..., :MLA_NOPE], _rope_lanes(r[..., MLA_NOPE:])], axis=-1)
    return out.reshape(w.shape[:-1] + (MLA_HEADS * 2 * V7X_LANES,)).astype(BF16)


def _rope_tables(S):
    half = MLA_ROPE // 2
    inv = ROPE_THETA ** (-jnp.arange(half, dtype=F32) / half)
    ang = jnp.arange(S, dtype=F32)[:, None] * inv[None, :]
    cos, sin = jnp.cos(ang), jnp.sin(ang)
    z = jnp.zeros_like(cos)
    return jnp.concatenate([cos, z, cos, z], axis=-1), jnp.concatenate([-sin, z, sin, z], axis=-1)


def kernel(x, attn_norm_g, w_in, mla_q_lora_norm_g, mla_kv_lora_norm_g, mla_w_uq, mla_w_uk, mla_w_uv, mla_q_gain, mla_k_gain, diff_q_gain, diff_k_gain, diff_lam_q1, diff_lam_k1, diff_lam_q2, diff_lam_k2, diff_out_norm_g, dil_q_gain, dil_k_gain, gate_bias, w_br_mla, w_br_diff, w_br_dil, w_out, ffn_norm_g, ffn_w_up, ffn_conv_w, ffn_conv_b, ffn_w_down):
    B, S, D = x.shape
    depth = w_in.shape[0]
    T = B * S
    col, n_used = _z_columns(D)
    n_z = -(-n_used // 1024) * 1024
    cos_t, sin_t = _rope_tables(S)
    slopes = LOG2E * 2.0 ** (-8.0 * jnp.arange(1, DIFF_HEADS + DIL_HEADS + 1, dtype=F32) / (DIFF_HEADS + DIL_HEADS))
    s_mla = LOG2E * (MLA_NOPE + MLA_ROPE) ** -0.5
    rows = lambda v: v.reshape(depth, 1, -1).astype(F32)

    w_in_p = _w_in_layout(w_in, D, n_z, tk=128)
    w_uq_p = _w_uq_layout(mla_w_uq)
    w_uk_p = mla_w_uk.astype(BF16)
    w_uvt_p = jnp.swapaxes(mla_w_uv, 1, 2).astype(BF16)
    w_bra, w_brb, w_brc, w_out_p = (w.astype(BF16) for w in (w_br_mla, w_br_diff, w_br_dil, w_out))
    w_up_p, w_down_p = ffn_w_up.astype(BF16), ffn_w_down.astype(BF16)
    vecs = {
        "gql": rows(mla_q_lora_norm_g), "gkvl": rows(mla_kv_lora_norm_g),
        "gqn": rows(mla_q_gain[:, :MLA_NOPE] * s_mla), "gqr": rows(_rope_lanes(mla_q_gain[:, MLA_NOPE:]) * s_mla),
        "gkn": rows(mla_k_gain[:, :MLA_NOPE]), "gkr": rows(_rope_lanes(mla_k_gain[:, MLA_NOPE:])),
        "gdq": rows(jnp.tile(diff_q_gain, (1, 2)) * (LOG2E * DIFF_D ** -0.5)), "gdk": rows(jnp.tile(diff_k_gain, (1, 2))),
        "glq": rows(dil_q_gain * (LOG2E * DIL_D ** -0.5)), "glk": rows(dil_k_gain),
    }
    g_attn, g_ffn, g_bias, c_b = rows(attn_norm_g), rows(ffn_norm_g), rows(gate_bias), rows(ffn_conv_b)
    lam_q1, lam_k1, lam_q2, lam_k2 = rows(diff_lam_q1), rows(diff_lam_k1), rows(diff_lam_q2), rows(diff_lam_k2)
    g_dout = diff_out_norm_g.reshape(depth, DIFF_V, 1).astype(F32)
    sc_dil = jnp.concatenate([slopes[1::2], jnp.zeros((2,), F32)])

    xf = x.reshape(T, D)
    b3 = lambda a: a.reshape(B, S, a.shape[-1])
    flat = lambda a: a.reshape(T, a.shape[-1])
    for l in range(depth):
        lam_init = 0.8 - 0.6 * math.exp(-0.3 * l)
        z = _norm_matmul(xf, g_attn, w_in_p, l, tm=1024, tn=2048, out_dtype=BF16)
        qm, km, vmt, qd, kd, vdt, ql, kl, vl = _prep(
            z, cos_t, sin_t, vecs, w_uq_p, w_uk_p, w_uvt_p, l, S=S, tm=256, col=col)
        o_mla = _mla_attn(b3(qm), b3(km), vmt, blk=512)
        sc_diff = jnp.concatenate([slopes[0::2], jnp.full((2,), lam_init, F32)])
        o_diff = _diff_attn(sc_diff, b3(qd), b3(kd), vdt, lam_q1, lam_k1, lam_q2, lam_k2, g_dout, l, blk=512)
        o_dil = _dil_attn(sc_dil, b3(ql), b3(kl), b3(vl))
        xf = _merge(xf, z, g_bias, flat(o_mla), flat(o_diff), flat(o_dil), w_bra, w_brb, w_brc, w_out_p, l, tm=256)
        act = _ffn_up(xf, g_ffn, w_up_p, ffn_conv_w, c_b, l, S=S, tm=1024, tn=512)
        xf = _ffn_down(xf, act, w_down_p, l, tm=256)
    return xf.reshape(B, S, D)
```
